```python
import math
import jax, jax.numpy as jnp
from jax import lax
import numpy as np

D_MODEL = 1024
BATCH = 4
SEQ = 4096
DEPTH = 2
DEC_BATCH = 128
DEC_SEQ = 8
PAST_LEN = 2048
PAGE_SIZE = 128

FOX_HEADS = 8
FOX_HEAD_DIM = 64
FOX_WIDTH = FOX_HEADS * FOX_HEAD_DIM
FOX_BLOCK = 128
GDN_HEADS = 4
GDN_HEAD_DIM = 128
GDN_WIDTH = GDN_HEADS * GDN_HEAD_DIM
GDN_CONV_DIM = 3 * GDN_WIDTH
CONV_WIDTH = 4
GDN_CHUNK = 64
MIX_WIDTH = FOX_WIDTH + GDN_WIDTH
N_MEM = 256
MEM_HEADS = 4
MEM_HEAD_DIM = 128
MEM_WIDTH = MEM_HEADS * MEM_HEAD_DIM
D_FF = -(-8 * D_MODEL // (3 * 256)) * 256
EPS = 1e-6
IN_SPLITS = (FOX_WIDTH, FOX_WIDTH, FOX_WIDTH, FOX_HEADS, GDN_CONV_DIM, GDN_HEADS, GDN_HEADS, GDN_WIDTH)
IN_COLS = sum(IN_SPLITS)

kernel_name = 'fox_gdn_parallel_heads_step'


def rmsnorm(x, g):
    xf = x.astype(jnp.float32)
    xf = xf * lax.rsqrt(jnp.mean(xf * xf, axis=-1, keepdims=True) + EPS)
    return xf.astype(x.dtype) * g


def l2norm(x):
    return x * lax.rsqrt(jnp.sum(x * x, axis=-1, keepdims=True) + EPS)


def project_mix(x, g_norm, w_in, b_f):
    b, l, _ = x.shape
    n = rmsnorm(x, g_norm)
    offs = np.cumsum(IN_SPLITS)[:-1].tolist()
    fq, fk, fv, ff, qkv, ga, gb, gg = jnp.split(n @ w_in, offs, axis=-1)
    heads = lambda t: t.reshape(b, l, FOX_HEADS, FOX_HEAD_DIM)
    logf = jax.nn.log_sigmoid((ff + b_f).astype(jnp.float32))
    return heads(fq), heads(fk), heads(fv), logf, qkv, ga, gb, gg


def fox_attend(q, k, v, c_q, c_k, pos_q, pos_k):
    s = jnp.einsum('bqhd,bkhd->bhqk', q, k).astype(jnp.float32) * (FOX_HEAD_DIM ** -0.5)
    s = s + jnp.swapaxes(c_q, 1, 2)[..., :, None] - jnp.swapaxes(c_k, 1, 2)[..., None, :]
    s = jnp.where(pos_k[None, :] <= pos_q[:, None], s, -jnp.inf)
    p = jax.nn.softmax(s, axis=-1).astype(v.dtype)
    return jnp.einsum('bhqk,bkhd->bqhd', p, v)


def fox_prompt(q, k, v, c, pos):
    b, l, h, d = q.shape
    nb = l // FOX_BLOCK
    qb = q.reshape(b, nb, FOX_BLOCK, h, d).transpose(1, 0, 2, 3, 4)
    cb = c.reshape(b, nb, FOX_BLOCK, h).transpose(1, 0, 2, 3)
    pb = pos.reshape(nb, FOX_BLOCK)
    out = lax.map(lambda a: fox_attend(a[0], k, v, a[1], c, a[2], pos), (qb, cb, pb))
    return out.transpose(1, 0, 2, 3, 4).reshape(b, l, h * d)


def gated_delta_chunked(q, k, v, g, beta, s0):
    b, l, h, dk = q.shape
    dv = v.shape[-1]
    cs = math.gcd(l, GDN_CHUNK)
    n = l // cs
    chunks = lambda t: jnp.moveaxis(t.reshape(b, n, cs, h, *t.shape[3:]), 3, 1)
    q, k, v, g, beta = chunks(q), chunks(k), chunks(v), chunks(g), chunks(beta)
    gc = jnp.cumsum(g, axis=-1)
    incl = jnp.tril(jnp.ones((cs, cs), bool))
    strict = jnp.tril(jnp.ones((cs, cs), bool), -1)
    decay = jnp.exp(jnp.where(incl, gc[..., :, None] - gc[..., None, :], -jnp.inf))
    kk = jnp.einsum('bhncd,bhnsd->bhncs', k, k)
    a_mat = jnp.where(strict, kk * decay * beta[..., :, None], 0.0)
    rhs = jnp.concatenate([v * beta[..., None], k * (beta * jnp.exp(gc))[..., None]], axis=-1)
    sol = lax.linalg.triangular_solve(a_mat, rhs, left_side=True, lower=True, unit_diagonal=True)
    u_base, k_cum = sol[..., :dv], sol[..., dv:]
    qk = jnp.einsum('bhncd,bhnsd->bhncs', q, k) * decay
    q_dec = q * jnp.exp(gc)[..., None]
    k_dec = k * jnp.exp(gc[..., -1:] - gc)[..., None]
    g_tot = jnp.exp(gc[..., -1])
    xs = tuple(jnp.moveaxis(t, 2, 0) for t in (u_base, k_cum, qk, q_dec, k_dec, g_tot))

    def step(s, inp):
        u_n, kc_n, qk_n, qd_n, kd_n, gt_n = inp
        u = u_n - jnp.einsum('bhcd,bhde->bhce', kc_n, s)
        o = jnp.einsum('bhcd,bhde->bhce', qd_n, s) + jnp.einsum('bhcs,bhse->bhce', qk_n, u)
        s = s * gt_n[..., None, None] + jnp.einsum('bhcd,bhce->bhde', kd_n, u)
        return s, o

    s_fin, o = lax.scan(step, s0, xs)
    o = jnp.transpose(o, (1, 0, 3, 2, 4)).reshape(b, l, h, dv)
    return o, s_fin


def gdn_mixer(qkv, ga, gb, gg, conv_buf, s0, conv_w, a_log, dt_bias, norm_w):
    b, l, _ = qkv.shape
    xp = jnp.concatenate([conv_buf.astype(qkv.dtype), qkv], axis=1)
    conv = sum(xp[:, w:w + l] * conv_w[w] for w in range(CONV_WIDTH))
    new_buf = xp[:, l:]
    act = jax.nn.silu(conv).astype(jnp.float32)
    q, k, v = [t.reshape(b, l, GDN_HEADS, GDN_HEAD_DIM) for t in jnp.split(act, 3, axis=-1)]
    q = l2norm(q) * (GDN_HEAD_DIM ** -0.5)
    k = l2norm(k)
    beta = jax.nn.sigmoid(gb.astype(jnp.float32))
    g = -jnp.exp(a_log.astype(jnp.float32)) * jax.nn.softplus(ga.astype(jnp.float32) + dt_bias.astype(jnp.float32))
    o, s_new = gated_delta_chunked(q, k, v, g, beta, s0.astype(jnp.float32))
    gate = jax.nn.silu(gg.reshape(b, l, GDN_HEADS, GDN_HEAD_DIM).astype(jnp.float32))
    o = rmsnorm(o, norm_w) * gate
    return o.reshape(b, l, GDN_WIDTH).astype(qkv.dtype), s_new.astype(qkv.dtype), new_buf


def mem_kv(mem, g_norm, w_kv):
    b, m, _ = mem.shape
    mk, mv = jnp.split(rmsnorm(mem, g_norm) @ w_kv, 2, axis=-1)
    return mk.reshape(b, m, MEM_HEADS, MEM_HEAD_DIM), mv.reshape(b, m, MEM_HEADS, MEM_HEAD_DIM)


def mem_attend(x, g_norm, w_q, w_o, mk, mv):
    b, l, _ = x.shape
    q = (rmsnorm(x, g_norm) @ w_q).reshape(b, l, MEM_HEADS, MEM_HEAD_DIM)
    s = jnp.einsum('bqhd,bkhd->bhqk', q, mk).astype(jnp.float32) * (MEM_HEAD_DIM ** -0.5)
    p = jax.nn.softmax(s, axis=-1).astype(mv.dtype)
    o = jnp.einsum('bhqk,bkhd->bqhd', p, mv).reshape(b, l, MEM_WIDTH)
    return o @ w_o


def swiglu(x, g_norm, w_in, w_out):
    a, u = jnp.split(rmsnorm(x, g_norm) @ w_in, 2, axis=-1)
    return (jax.nn.silu(a) * u) @ w_out


def setup_inputs(seed: int = 0) -> dict:
    key = jax.random.key(seed)
    ks = iter(jax.random.split(key, 40))
    f32 = jnp.float32
    nrm = lambda shape, scale: jax.random.normal(next(ks), shape, f32) * scale
    gain = lambda shape: 1.0 + nrm(shape, 0.01)
    n_pages = PAST_LEN // PAGE_SIZE
    n_used = DEC_BATCH * n_pages
    n_phys = n_used + n_used // 4
    page_table = jax.random.permutation(next(ks), n_phys)[:n_used].reshape(DEC_BATCH, n_pages).astype(jnp.int32)
    a_log = jnp.log(jax.random.uniform(next(ks), (DEPTH, GDN_HEADS), f32, 1.0, 16.0))
    dt = jnp.exp(jax.random.uniform(next(ks), (DEPTH, GDN_HEADS), f32, math.log(1e-3), math.log(1e-1)))
    dt_bias = dt + jnp.log(-jnp.expm1(-dt))
    return {
        'x_prompt': nrm((BATCH, SEQ, D_MODEL), 1.0),
        'x_sample': nrm((DEC_BATCH, DEC_SEQ, D_MODEL), 1.0),
        'cache_fox_k': nrm((DEPTH, n_phys, PAGE_SIZE, FOX_HEADS, FOX_HEAD_DIM), 1.0),
        'cache_fox_v': nrm((DEPTH, n_phys, PAGE_SIZE, FOX_HEADS, FOX_HEAD_DIM), 1.0),
        'cache_fox_logf': jax.nn.log_sigmoid(3.0 + nrm((DEPTH, n_phys, PAGE_SIZE, FOX_HEADS), 1.0)),
        'state_gdn': nrm((DEPTH, DEC_BATCH, GDN_HEADS, GDN_HEAD_DIM, GDN_HEAD_DIM), 0.05),
        'state_gdn_conv': nrm((DEPTH, DEC_BATCH, CONV_WIDTH - 1, GDN_CONV_DIM), 1.0),
        'cache_mem_k': nrm((DEPTH, DEC_BATCH, N_MEM, MEM_HEADS, MEM_HEAD_DIM), 1.0),
        'cache_mem_v': nrm((DEPTH, DEC_BATCH, N_MEM, MEM_HEADS, MEM_HEAD_DIM), 1.0),
        'page_table': page_table,
        'mem_prompt': nrm((BATCH, N_MEM, D_MODEL), 1.0),
        'g_norm_mix': gain((DEPTH, D_MODEL)),
        'w_in': nrm((DEPTH, D_MODEL, IN_COLS), D_MODEL ** -0.5),
        'b_fox_f': 3.0 + nrm((DEPTH, FOX_HEADS), 0.5),
        'gdn_conv_w': nrm((DEPTH, CONV_WIDTH, GDN_CONV_DIM), CONV_WIDTH ** -0.5),
        'gdn_a_log': a_log,
        'gdn_dt_bias': dt_bias,
        'gdn_norm_w': gain((DEPTH, GDN_HEAD_DIM)),
        'w_out': nrm((DEPTH, MIX_WIDTH, D_MODEL), MIX_WIDTH ** -0.5),
        'g_norm_memin': gain((DEPTH, D_MODEL)),
        'w_mem_kv': nrm((DEPTH, D_MODEL, 2 * MEM_WIDTH), D_MODEL ** -0.5),
        'g_norm_mem': gain((DEPTH, D_MODEL)),
        'w_mem_q': nrm((DEPTH, D_MODEL, MEM_WIDTH), D_MODEL ** -0.5),
        'w_mem_o': nrm((DEPTH, MEM_WIDTH, D_MODEL), MEM_WIDTH ** -0.5),
        'g_norm_ffn': gain((DEPTH, D_MODEL)),
        'w_ffn_in': nrm((DEPTH, D_MODEL, 2 * D_FF), D_MODEL ** -0.5),
        'w_ffn_out': nrm((DEPTH, D_FF, D_MODEL), D_FF ** -0.5),
        'g_final': gain((D_MODEL,)),
    }


def reference(x_prompt, x_sample, cache_fox_k, cache_fox_v, cache_fox_logf, state_gdn, state_gdn_conv,
              cache_mem_k, cache_mem_v, page_table, mem_prompt, g_norm_mix, w_in, b_fox_f, gdn_conv_w,
              gdn_a_log, gdn_dt_bias, gdn_norm_w, w_out, g_norm_memin, w_mem_kv, g_norm_mem, w_mem_q,
              w_mem_o, g_norm_ffn, w_ffn_in, w_ffn_out, g_final):
    bp, lp, _ = x_prompt.shape
    bs, ls, _ = x_sample.shape
    n_pages = page_table.shape[1]
    past_len = n_pages * cache_fox_k.shape[2]
    pos_prompt = jnp.arange(lp)
    pos_q_sample = past_len + jnp.arange(ls)
    pos_k_sample = jnp.arange(past_len + ls)
    yp, ys = x_prompt, x_sample
    fkp, fvp, flp, gsp, gcp, mkp, mvp = [], [], [], [], [], [], []
    fks, fvs, fls, gss, gcs = [], [], [], [], []
    for l in range(DEPTH):
        gdn_p = (gdn_conv_w[l], gdn_a_log[l], gdn_dt_bias[l], gdn_norm_w[l])
        fq, fk, fv, logf, qkv, ga, gb, gg = project_mix(yp, g_norm_mix[l], w_in[l], b_fox_f[l])
        fo = fox_prompt(fq, fk, fv, jnp.cumsum(logf, axis=1), pos_prompt)
        go, s_new, buf_new = gdn_mixer(qkv, ga, gb, gg,
                                       jnp.zeros((bp, CONV_WIDTH - 1, GDN_CONV_DIM), qkv.dtype),
                                       jnp.zeros((bp, GDN_HEADS, GDN_HEAD_DIM, GDN_HEAD_DIM), jnp.float32), *gdn_p)
        yp = yp + jnp.concatenate([fo, go], axis=-1) @ w_out[l]
        mk, mv = mem_kv(mem_prompt, g_norm_memin[l], w_mem_kv[l])
        yp = yp + mem_attend(yp, g_norm_mem[l], w_mem_q[l], w_mem_o[l], mk, mv)
        yp = yp + swiglu(yp, g_norm_ffn[l], w_ffn_in[l], w_ffn_out[l])
        fkp.append(fk); fvp.append(fv); flp.append(logf.astype(fk.dtype))
        gsp.append(s_new); gcp.append(buf_new); mkp.append(mk); mvp.append(mv)
        fq, fk, fv, logf, qkv, ga, gb, gg = project_mix(ys, g_norm_mix[l], w_in[l], b_fox_f[l])
        k_all = jnp.concatenate([cache_fox_k[l][page_table].reshape(bs, past_len, FOX_HEADS, FOX_HEAD_DIM), fk], axis=1)
        v_all = jnp.concatenate([cache_fox_v[l][page_table].reshape(bs, past_len, FOX_HEADS, FOX_HEAD_DIM), fv], axis=1)
        lf_all = jnp.concatenate([cache_fox_logf[l][page_table].reshape(bs, past_len, FOX_HEADS).astype(jnp.float32), logf], axis=1)
        c_all = jnp.cumsum(lf_all, axis=1)
        fo = fox_attend(fq, k_all, v_all, c_all[:, past_len:], c_all, pos_q_sample, pos_k_sample).reshape(bs, ls, FOX_WIDTH)
        go, s_new, buf_new = gdn_mixer(qkv, ga, gb, gg, state_gdn_conv[l], state_gdn[l], *gdn_p)
        ys = ys + jnp.concatenate([fo, go], axis=-1) @ w_out[l]
        ys = ys + mem_attend(ys, g_norm_mem[l], w_mem_q[l], w_mem_o[l], cache_mem_k[l], cache_mem_v[l])
        ys = ys + swiglu(ys, g_norm_ffn[l], w_ffn_in[l], w_ffn_out[l])
        fks.append(fk); fvs.append(fv); fls.append(logf.astype(fk.dtype))
        gss.append(s_new); gcs.append(buf_new)
    yp = rmsnorm(yp, g_final)
    ys = rmsnorm(ys, g_final)
    return (yp, ys,
            jnp.stack(fkp), jnp.stack(fvp), jnp.stack(flp), jnp.stack(gsp), jnp.stack(gcp), jnp.stack(mkp), jnp.stack(mvp),
            jnp.stack(fks), jnp.stack(fvs), jnp.stack(fls), jnp.stack(gss), jnp.stack(gcs))
```

```python
import functools
import math

import jax
import jax.numpy as jnp
from jax import lax
from jax.experimental import pallas as pl
from jax.experimental.pallas import tpu as pltpu

F32 = jnp.float32
BF16 = jnp.bfloat16
HI = lax.Precision.HIGHEST

EPS = 1e-6
D_MODEL = 1024
FOX_HEADS = 8
FOX_HEAD_DIM = 64
FOX_WIDTH = FOX_HEADS * FOX_HEAD_DIM
GDN_HEADS = 4
GDN_HEAD_DIM = 128
GDN_WIDTH = GDN_HEADS * GDN_HEAD_DIM
GDN_CONV_DIM = 3 * GDN_WIDTH
CONV_WIDTH = 4
GDN_CHUNK = 64
MEM_HEADS = 4
MEM_HEAD_DIM = 128
MEM_WIDTH = MEM_HEADS * MEM_HEAD_DIM
LANES = 128
SUBLANES = 8
SMALL_COLS = 16
GA_COL = FOX_HEADS
GB_COL = FOX_HEADS + GDN_HEADS
VMEM_LIMIT = 56 * 1024 * 1024


def _params(*sem):
    return pltpu.CompilerParams(dimension_semantics=sem, vmem_limit_bytes=VMEM_LIMIT)


def _dot(a, b):
    return jnp.dot(a, b, preferred_element_type=F32)


def _dot_nt(a, b):
    return lax.dot_general(a, b, (((1,), (1,)), ((), ())), preferred_element_type=F32)


def _dot_tn(a, b):
    return lax.dot_general(a, b, (((0,), (0,)), ((), ())), preferred_element_type=F32)


def _dot_hi(a, b):
    return jnp.dot(a, b, precision=HI, preferred_element_type=F32)


def _dot_nt_hi(a, b):
    return lax.dot_general(a, b, (((1,), (1,)), ((), ())), precision=HI, preferred_element_type=F32)


def _rms(x, g):
    return x * lax.rsqrt(jnp.mean(x * x, axis=-1, keepdims=True) + EPS) * g


def _log_sigmoid(x):
    return jnp.minimum(x, 0.0) - jnp.log1p(jnp.exp(-jnp.abs(x)))


def _softplus(x):
    return jnp.maximum(x, 0.0) + jnp.log1p(jnp.exp(-jnp.abs(x)))


def _silu(x):
    return x * jax.nn.sigmoid(x)


def _iota(shape, dim):
    return lax.broadcasted_iota(jnp.int32, shape, dim)


def _log2(n):
    lg = n.bit_length() - 1
    assert n == 1 << lg, n
    return lg


def _div(x, n):
    return x >> _log2(n)


def _mod(x, n):
    _log2(n)
    return x & (n - 1)


def _in_proj_common(x_ref, g_ref, wq_ref, wg_ref, ws_ref, wst_ref, b_ref,
                    fqb_ref, qkv_ref, gg_ref, lf_ref, sm_ref, smt_ref):
    xn = _rms(x_ref[...], g_ref[...]).astype(BF16)
    fqb_ref[...] = (_dot(xn, wq_ref[...]) * (FOX_HEAD_DIM ** -0.5)).astype(BF16)
    qkv_ref[...] = _dot(xn, wg_ref[:, 0:GDN_CONV_DIM])
    gg_ref[...] = _dot(xn, wg_ref[:, GDN_CONV_DIM:])
    sm = _dot(xn, ws_ref[...])
    smt = _dot_nt(wst_ref[...], xn)
    sm_ref[...] = sm
    smt_ref[...] = smt
    lf_ref[...] = _log_sigmoid(sm[:, 0:FOX_HEADS] + b_ref[...])
    return xn, smt


def _in_proj_prompt_kernel(x_ref, g_ref, wq_ref, wkvt_ref, wg_ref, ws_ref, wst_ref, b_ref, bcol_ref, *rest):
    (fqb_ref, kt_ref, vt_ref, ktb_ref, vtb_ref, qkv_ref, gg_ref, lf_ref, lft_ref, sm_ref, smt_ref) = rest[-11:]
    xn, smt = _in_proj_common(x_ref, g_ref, wq_ref, wg_ref, ws_ref, wst_ref, b_ref,
                              fqb_ref, qkv_ref, gg_ref, lf_ref, sm_ref, smt_ref)
    kt = _dot_nt(wkvt_ref[0:FOX_WIDTH, :], xn)
    kt_ref[0, 0] = kt
    ktb_ref[0] = kt.astype(BF16)
    vt = _dot_nt(wkvt_ref[FOX_WIDTH:, :], xn)
    vt_ref[0, 0] = vt
    vtb_ref[0] = vt.astype(BF16)
    lft_ref[0] = _log_sigmoid(smt[0:FOX_HEADS, :] + bcol_ref[...])


def _in_proj_prompt(layer, depth, batch, x, g, wq, wkvt, wg, ws, wst, b_row, b_col, prev_kt, prev_vt, tm):
    m = x.shape[0]
    l = m // batch
    nt = l // tm
    row = lambda n: pl.BlockSpec((tm, n), lambda i: (i, 0))
    full = lambda a: pl.BlockSpec(a.shape, lambda i: (0,) * a.ndim)
    per_b = lambda rows: pl.BlockSpec((1, rows, tm), lambda i: (i // nt, 0, i % nt))
    kv_all = pl.BlockSpec((1, 1, FOX_WIDTH, tm), lambda i: (layer, i // nt, 0, i % nt))
    out_shape = (
        jax.ShapeDtypeStruct((m, FOX_WIDTH), BF16),
        jax.ShapeDtypeStruct((depth, batch, FOX_WIDTH, l), F32),
        jax.ShapeDtypeStruct((depth, batch, FOX_WIDTH, l), F32),
        jax.ShapeDtypeStruct((batch, FOX_WIDTH, l), BF16),
        jax.ShapeDtypeStruct((batch, FOX_WIDTH, l), BF16),
        jax.ShapeDtypeStruct((m, GDN_CONV_DIM), F32),
        jax.ShapeDtypeStruct((m, GDN_WIDTH), F32),
        jax.ShapeDtypeStruct((m, FOX_HEADS), F32),
        jax.ShapeDtypeStruct((batch, FOX_HEADS, l), F32),
        jax.ShapeDtypeStruct((m, LANES), F32),
        jax.ShapeDtypeStruct((SMALL_COLS, m), F32),
    )
    out_specs = (row(FOX_WIDTH), kv_all, kv_all, per_b(FOX_WIDTH), per_b(FOX_WIDTH),
                 row(GDN_CONV_DIM), row(GDN_WIDTH), row(FOX_HEADS), per_b(FOX_HEADS), row(LANES),
                 pl.BlockSpec((SMALL_COLS, tm), lambda i: (0, i)))
    args = [x, g, wq, wkvt, wg, ws, wst, b_row, b_col]
    in_specs = [row(D_MODEL)] + [full(a) for a in args[1:]]
    aliases = {}
    if prev_kt is not None:
        aliases = {len(args): 1, len(args) + 1: 2}
        args += [prev_kt, prev_vt]
        in_specs += [pl.BlockSpec(memory_space=pl.ANY)] * 2
    return pl.pallas_call(
        _in_proj_prompt_kernel, grid=(m // tm,), in_specs=in_specs, out_specs=out_specs, out_shape=out_shape,
        input_output_aliases=aliases, compiler_params=_params("arbitrary"), name="in_proj_prompt")(*args)


def _in_proj_sample_kernel(x_ref, g_ref, wq_ref, wkv_ref, wg_ref, ws_ref, wst_ref, b_ref,
                           fqb_ref, fk_ref, fv_ref, qkv_ref, gg_ref, lf_ref, sm_ref, smt_ref):
    xn, _ = _in_proj_common(x_ref, g_ref, wq_ref, wg_ref, ws_ref, wst_ref, b_ref,
                            fqb_ref, qkv_ref, gg_ref, lf_ref, sm_ref, smt_ref)
    fk_ref[...] = _dot(xn, wkv_ref[:, 0:FOX_WIDTH])
    fv_ref[...] = _dot(xn, wkv_ref[:, FOX_WIDTH:])


def _in_proj_sample(x, g, wq, wkv, wg, ws, wst, b_row, tm):
    m = x.shape[0]
    row = lambda n: pl.BlockSpec((tm, n), lambda i: (i, 0))
    full = lambda a: pl.BlockSpec(a.shape, lambda i: (0,) * a.ndim)
    out_shape = (
        jax.ShapeDtypeStruct((m, FOX_WIDTH), BF16),
        jax.ShapeDtypeStruct((m, FOX_WIDTH), F32),
        jax.ShapeDtypeStruct((m, FOX_WIDTH), F32),
        jax.ShapeDtypeStruct((m, GDN_CONV_DIM), F32),
        jax.ShapeDtypeStruct((m, GDN_WIDTH), F32),
        jax.ShapeDtypeStruct((m, FOX_HEADS), F32),
        jax.ShapeDtypeStruct((m, LANES), F32),
        jax.ShapeDtypeStruct((SMALL_COLS, m), F32),
    )
    out_specs = (row(FOX_WIDTH), row(FOX_WIDTH), row(FOX_WIDTH), row(GDN_CONV_DIM), row(GDN_WIDTH),
                 row(FOX_HEADS), row(LANES), pl.BlockSpec((SMALL_COLS, tm), lambda i: (0, i)))
    args = [x, g, wq, wkv, wg, ws, wst, b_row]
    return pl.pallas_call(
        _in_proj_sample_kernel, grid=(m // tm,), in_specs=[row(D_MODEL)] + [full(a) for a in args[1:]],
        out_specs=out_specs, out_shape=out_shape, compiler_params=_params("arbitrary"),
        name="in_proj_sample")(*args)


def _cumsum_tokens_kernel(x_ref, o_ref, *, seg_rows):
    x = x_ref[...]
    rows = x.shape[0]
    a = _iota((LANES, LANES), 0)
    b = _iota((LANES, LANES), 1)
    same_head = (a & (FOX_HEADS - 1)) == (b & (FOX_HEADS - 1))
    upto = (same_head & ((a >> 3) <= (b >> 3))).astype(F32)
    every = same_head.astype(F32)
    r = _iota((rows, rows), 0)
    c = _iota((rows, rows), 1)
    sh = _log2(seg_rows)
    before = ((c < r) & ((c >> sh) == (r >> sh))).astype(F32)
    o_ref[...] = _dot_hi(x, upto) + _dot_hi(before, _dot_hi(x, every))


def _cumsum_tokens(x, seg_rows, block_rows):
    rows = x.shape[0]
    spec = pl.BlockSpec((block_rows, LANES), lambda i: (i, 0))
    return pl.pallas_call(
        functools.partial(_cumsum_tokens_kernel, seg_rows=seg_rows), grid=(rows // block_rows,),
        in_specs=[spec], out_specs=spec, out_shape=jax.ShapeDtypeStruct(x.shape, F32),
        compiler_params=_params("arbitrary"), name="cumsum_tokens")(x)


def _cumsum_lanes_kernel(x_ref, o_ref, carry_ref):
    @pl.when(pl.program_id(1) == 0)
    def _():
        carry_ref[...] = jnp.zeros_like(carry_ref)
    x = x_ref[0]
    n = x.shape[1]
    upto = (_iota((n, n), 0) <= _iota((n, n), 1)).astype(F32)
    y = _dot_hi(x, upto) + carry_ref[...]
    o_ref[0] = y
    carry_ref[...] = y[:, n - 1:n]


def _cumsum_lanes(xt, chunk):
    batch, heads, l = xt.shape
    spec = pl.BlockSpec((1, heads, chunk), lambda b, j: (b, 0, j))
    return pl.pallas_call(
        _cumsum_lanes_kernel, grid=(batch, l // chunk), in_specs=[spec], out_specs=spec,
        out_shape=jax.ShapeDtypeStruct(xt.shape, F32),
        scratch_shapes=[pltpu.VMEM((heads, 1), F32)],
        compiler_params=_params("arbitrary", "arbitrary"), name="cumsum_lanes")(xt)


def _cumsum_pages_kernel(x_ref, o_ref):
    n = x_ref.shape[1]
    upto = (_iota((n, n), 0) <= _iota((n, n), 1)).astype(F32)
    o_ref[...] = _dot_hi(x_ref[...], upto)


def _cumsum_pages(x, block_rows):
    rows, n = x.shape
    spec = pl.BlockSpec((block_rows, n), lambda i: (i, 0))
    return pl.pallas_call(
        _cumsum_pages_kernel, grid=(rows // block_rows,), in_specs=[spec], out_specs=spec,
        out_shape=jax.ShapeDtypeStruct(x.shape, F32), compiler_params=_params("arbitrary"),
        name="cumsum_pages")(x)


def _fox_prompt_kernel(q_ref, kt_ref, vt_ref, cq_ref, ck_ref, o_ref, m_ref, l_ref, acc_ref, *, tq, tk):
    qi = pl.program_id(1)
    ki = pl.program_id(2)

    @pl.when(ki == 0)
    def _():
        m_ref[...] = jnp.full_like(m_ref, -jnp.inf)
        l_ref[...] = jnp.zeros_like(l_ref)
        acc_ref[...] = jnp.zeros_like(acc_ref)

    low = _iota((tq, LANES), 1) < FOX_HEAD_DIM

    def step(diagonal):
        if diagonal:
            visible = _iota((tq, tk), 1) <= _iota((tq, tk), 0)
        for j in range(FOX_HEADS // 2):
            cols = slice(j * LANES, (j + 1) * LANES)
            qp = q_ref[0, :, cols]
            kp = kt_ref[0, cols, :]
            vp = vt_ref[0, cols, :]
            alphas, outs = [], []
            for half in range(2):
                h = 2 * j + half
                qh = jnp.where(low if half == 0 else ~low, qp, jnp.zeros_like(qp))
                t = _dot(qh, kp) - ck_ref[0, h:h + 1, :]
                if diagonal:
                    t = jnp.where(visible, t, -jnp.inf)
                cq = cq_ref[0, :, h:h + 1]
                m_old = m_ref[h]
                m_new = jnp.maximum(m_old, jnp.max(t, axis=1, keepdims=True) + cq)
                alpha = jnp.exp(m_old - m_new)
                p = jnp.exp(t + (cq - m_new))
                l_ref[h] = alpha * l_ref[h] + jnp.sum(p, axis=1, keepdims=True)
                m_ref[h] = m_new
                alphas.append(alpha)
                outs.append(_dot_nt(p.astype(BF16), vp))
            acc_ref[j] = (jnp.where(low, alphas[0], alphas[1]) * acc_ref[j]
                          + jnp.where(low, outs[0], outs[1]))

    @pl.when(ki < qi)
    def _():
        step(False)

    @pl.when(ki == qi)
    def _():
        step(True)
        for j in range(FOX_HEADS // 2):
            inv = jnp.where(low, 1.0 / l_ref[2 * j], 1.0 / l_ref[2 * j + 1])
            o_ref[0, :, j * LANES:(j + 1) * LANES] = acc_ref[j] * inv


def _fox_prompt(qb, ktb, vtb, c_col, c_row, tq):
    b, l, _ = qb.shape
    tk = tq
    n = l // tq
    kv_spec = pl.BlockSpec((1, FOX_WIDTH, tk), lambda bi, qi, ki: (bi, 0, jnp.minimum(ki, qi)))
    return pl.pallas_call(
        functools.partial(_fox_prompt_kernel, tq=tq, tk=tk), grid=(b, n, n),
        in_specs=[pl.BlockSpec((1, tq, FOX_WIDTH), lambda bi, qi, ki: (bi, qi, 0)), kv_spec, kv_spec,
                  pl.BlockSpec((1, tq, FOX_HEADS), lambda bi, qi, ki: (bi, qi, 0)),
                  pl.BlockSpec((1, FOX_HEADS, tk), lambda bi, qi, ki: (bi, 0, jnp.minimum(ki, qi)))],
        out_specs=pl.BlockSpec((1, tq, FOX_WIDTH), lambda bi, qi, ki: (bi, qi, 0)),
        out_shape=jax.ShapeDtypeStruct((b, l, FOX_WIDTH), F32),
        scratch_shapes=[pltpu.VMEM((FOX_HEADS, tq, 1), F32), pltpu.VMEM((FOX_HEADS, tq, 1), F32),
                        pltpu.VMEM((FOX_HEADS // 2, tq, LANES), F32)],
        compiler_params=_params("arbitrary", "arbitrary", "arbitrary"), name="fox_prompt")(qb, ktb, vtb, c_col, c_row)


def _fox_decode_kernel(pt_ref, q_ref, kn_ref, vn_ref, sm_ref, b_ref, *rest, pages_per_step, page, ls):
    del pt_ref
    r_ = pages_per_step
    k_refs, v_refs, c_refs = rest[0:r_], rest[r_:2 * r_], rest[2 * r_:3 * r_]
    o_ref, m_ref, l_ref, acc_ref, suffix_ref = rest[3 * r_:]
    rows = FOX_HEADS * ls
    g = pl.program_id(1)
    headmask = _div(_iota((rows, FOX_WIDTH), 0), ls) == _div(_iota((rows, FOX_WIDTH), 1), FOX_HEAD_DIM)
    expand = (_div(_iota((rows, FOX_HEADS), 0), ls) == _iota((rows, FOX_HEADS), 1)).astype(F32)
    q = q_ref[0].astype(F32)
    qbd = jnp.where(headmask, jnp.concatenate([q] * FOX_HEADS, axis=0), 0.0).astype(BF16)

    lane = _iota((ls, LANES), 1)
    lf_new = jnp.where(lane < FOX_HEADS, _log_sigmoid(sm_ref[0] + b_ref[...]), 0.0)
    tok = _iota((ls, LANES), 0)
    cum_new = lf_new
    for s in range(1, ls):
        cum_new = cum_new + jnp.where(tok >= s, pltpu.roll(lf_new, s, 0), 0.0)
    cum_new = cum_new[:, 0:FOX_HEADS]
    query_of_row = (_mod(_iota((rows, ls), 0), ls) == _iota((rows, ls), 1)).astype(F32)
    cq_row = jnp.sum(_dot_hi(query_of_row, cum_new) * expand, axis=1, keepdims=True)

    def update(t, rowconst, pv):
        m_old = m_ref[...]
        m_new = jnp.maximum(m_old, jnp.max(t, axis=1, keepdims=True) + rowconst)
        alpha = jnp.exp(m_old - m_new)
        p = jnp.exp(t + (rowconst - m_new))
        l_ref[...] = alpha * l_ref[...] + jnp.sum(p, axis=1, keepdims=True)
        m_ref[...] = m_new
        acc_ref[...] = alpha * acc_ref[...] + pv(p.astype(BF16))

    @pl.when(g == 0)
    def _():
        m_ref[...] = jnp.full_like(m_ref, -jnp.inf)
        l_ref[...] = jnp.zeros_like(l_ref)
        acc_ref[...] = jnp.zeros_like(acc_ref)
        suffix_ref[...] = jnp.zeros_like(suffix_ref)
        s_new = _dot_nt(qbd, kn_ref[0].astype(BF16))
        t = s_new - _dot_nt_hi(expand, cum_new)
        vis = _iota((rows, ls), 1) <= _mod(_iota((rows, ls), 0), ls)
        v_new = vn_ref[0].astype(BF16)
        update(jnp.where(vis, t, -jnp.inf), cq_row, lambda p: _dot(p, v_new))

    for r in range(r_):
        cum_page = c_refs[r][0, 0]
        reach = suffix_ref[...] + cum_page[:, page - 1:page]
        t = _dot(qbd, k_refs[r][0, 0].astype(BF16)) - _dot_hi(expand, cum_page - reach)
        vt = v_refs[r][0, 0].astype(BF16)
        update(t, cq_row, lambda p: _dot_nt(p, vt))
        suffix_ref[...] = reach

    @pl.when(g == pl.num_programs(1) - 1)
    def _():
        full = jnp.where(headmask, acc_ref[...] / l_ref[...], 0.0)
        o_ref[0] = jnp.sum(full.reshape(FOX_HEADS, ls, FOX_WIDTH), axis=0)


def _fox_decode(layer, page_table, qb, k_new, v_new, sm, b_pad, cache_k, cache_v, cache_cum, pages_per_step):
    bs, ls, _ = qb.shape
    n_pages = page_table.shape[1]
    page = cache_k.shape[3]
    steps = n_pages // pages_per_step

    def page_map(r):
        return lambda b, g, pt: (layer, pt[b, n_pages - 1 - (g * pages_per_step + r)], 0, 0)

    per_sample = lambda n: pl.BlockSpec((1, ls, n), lambda b, g, pt: (b, 0, 0))
    in_specs = [per_sample(FOX_WIDTH), per_sample(FOX_WIDTH), per_sample(FOX_WIDTH), per_sample(LANES),
                pl.BlockSpec((1, LANES), lambda b, g, pt: (0, 0))]
    in_specs += [pl.BlockSpec((1, 1, FOX_WIDTH, page), page_map(r)) for r in range(pages_per_step)]
    in_specs += [pl.BlockSpec((1, 1, FOX_WIDTH, page), page_map(r)) for r in range(pages_per_step)]
    in_specs += [pl.BlockSpec((1, 1, FOX_HEADS, page), page_map(r)) for r in range(pages_per_step)]
    rows = FOX_HEADS * ls
    grid_spec = pltpu.PrefetchScalarGridSpec(
        num_scalar_prefetch=1, grid=(bs, steps), in_specs=in_specs,
        out_specs=pl.BlockSpec((1, ls, FOX_WIDTH), lambda b, g, pt: (b, 0, 0)),
        scratch_shapes=[pltpu.VMEM((rows, 1), F32), pltpu.VMEM((rows, 1), F32),
                        pltpu.VMEM((rows, FOX_WIDTH), F32), pltpu.VMEM((FOX_HEADS, 1), F32)])
    kern = functools.partial(_fox_decode_kernel, pages_per_step=pages_per_step, page=page, ls=ls)
    args = [page_table, qb, k_new, v_new, sm, b_pad]
    args += [cache_k] * pages_per_step + [cache_v] * pages_per_step + [cache_cum] * pages_per_step
    return pl.pallas_call(
        kern, grid_spec=grid_spec, out_shape=jax.ShapeDtypeStruct((bs, ls, FOX_WIDTH), F32),
        compiler_params=_params("arbitrary", "arbitrary"), name="fox_decode")(*args)


def _gdn_tile(h, conv, sm, smt, gg_ref, cw_ref, alog_row, dtb_row, alog_col, dtb_col, nw_ref, cs):
    t_rows = sm.shape[0]
    lg = _log2(cs)
    hd = GDN_HEAD_DIM
    act = lambda base: _silu(conv(slice(base + h * hd, base + (h + 1) * hd)))
    l2 = lambda x: x * lax.rsqrt(jnp.sum(x * x, axis=-1, keepdims=True) + EPS)
    q = l2(act(0)) * (hd ** -0.5)
    k = l2(act(GDN_WIDTH))
    v = act(2 * GDN_WIDTH)

    r = _iota((t_rows, t_rows), 0)
    c = _iota((t_rows, t_rows), 1)
    same = (r >> lg) == (c >> lg)
    incl = same & (c <= r)
    strict = same & (c < r)

    g_all = -jnp.exp(alog_row) * _softplus(sm + dtb_row)
    beta = jax.nn.sigmoid(sm[:, GB_COL + h:GB_COL + h + 1])
    g_col1 = g_all[:, GA_COL + h:GA_COL + h + 1]
    gc = _dot_hi(incl.astype(F32), g_col1)
    gl = _dot_hi(same.astype(F32), g_col1)
    g_row_all = -jnp.exp(alog_col) * _softplus(smt + dtb_col)
    g_row1 = g_row_all[GA_COL + h:GA_COL + h + 1, :]
    gc_row = _dot_hi(g_row1, (same & (r <= c)).astype(F32))

    decay = jnp.exp(jnp.where(incl, gc - gc_row, -jnp.inf))
    kb = k.astype(BF16)
    kk = _dot_nt(kb, kb)
    qk = _dot_nt(q.astype(BF16), kb) * decay
    neg_a = jnp.where(strict, -(kk * decay * beta), 0.0)
    eye = (r == c).astype(F32)
    inv = eye + neg_a
    power = neg_a
    for _ in range(lg - 1):
        power = _dot_hi(power, power)
        inv = inv + _dot_hi(inv, power)
    egc = jnp.exp(gc)
    rhs = jnp.concatenate([v * beta, k * (beta * egc)], axis=1)
    sol = _dot_hi(inv, rhs)
    u_base, k_cum = sol[:, :hd], sol[:, hd:]
    q_dec = q * egc
    k_dec = k * jnp.exp(gl - gc)
    return u_base, k_cum, qk, q_dec, k_dec, gl


def _gdn_finish(h, o, gg_ref, nw_ref, go_ref, lead):
    hd = GDN_HEAD_DIM
    cols = slice(h * hd, (h + 1) * hd)
    gate = _silu(gg_ref[lead + (slice(None), cols)])
    go_ref[lead + (slice(None), cols)] = _rms(o, nw_ref[...]) * gate


def _gdn_prompt_kernel(qkv_ref, sm_ref, smt_ref, gg_ref, cw_ref, alog_row, dtb_row, alog_col, dtb_col, nw_ref,
                       go_ref, s_out_ref, xbuf_ref, s_ref, *, cs):
    t = pl.program_id(1)
    t_rows = qkv_ref.shape[0]

    @pl.when(t == 0)
    def _():
        xbuf_ref[0:SUBLANES, :] = jnp.zeros((SUBLANES, GDN_CONV_DIM), F32)
        s_ref[...] = jnp.zeros_like(s_ref)

    xbuf_ref[SUBLANES:, :] = qkv_ref[...]

    def conv(cols):
        acc = xbuf_ref[SUBLANES:SUBLANES + t_rows, cols] * cw_ref[CONV_WIDTH - 1:CONV_WIDTH, cols]
        for w in range(CONV_WIDTH - 1):
            lo = SUBLANES - (CONV_WIDTH - 1) + w
            acc = acc + xbuf_ref[lo:lo + t_rows, cols] * cw_ref[w:w + 1, cols]
        return acc

    sm = sm_ref[...]
    smt = smt_ref[...]
    for h in range(GDN_HEADS):
        u_base, k_cum, qk, q_dec, k_dec, gl = _gdn_tile(
            h, conv, sm, smt, gg_ref, cw_ref, alog_row[...], dtb_row[...], alog_col[...], dtb_col[...], nw_ref, cs)
        state = s_ref[h]
        us, ois = [], []
        for ci in range(t_rows // cs):
            rows = slice(ci * cs, (ci + 1) * cs)
            sb = state.astype(BF16)
            u = u_base[rows] - _dot(k_cum[rows].astype(BF16), sb)
            ois.append(_dot(q_dec[rows].astype(BF16), sb))
            state = state * jnp.exp(gl[ci * cs:ci * cs + 1, :]) + _dot_tn(k_dec[rows].astype(BF16), u.astype(BF16))
            us.append(u)
        s_ref[h] = state
        o = jnp.concatenate(ois, axis=0) + _dot(qk.astype(BF16), jnp.concatenate(us, axis=0).astype(BF16))
        _gdn_finish(h, o, gg_ref, nw_ref, go_ref, ())

    xbuf_ref[0:SUBLANES, :] = qkv_ref[t_rows - SUBLANES:, :]

    @pl.when(t == pl.num_programs(1) - 1)
    def _():
        s_out_ref[0] = s_ref[...]


def _gdn_prompt(qkv, sm, smt, gg, cw, alog_row, dtb_row, alog_col, dtb_col, nw, batch, tile):
    m = qkv.shape[0]
    steps = m // batch // tile
    row = lambda n: pl.BlockSpec((tile, n), lambda b, t: (b * steps + t, 0))
    full = lambda a: pl.BlockSpec(a.shape, lambda b, t: (0,) * a.ndim)
    hd = GDN_HEAD_DIM
    return pl.pallas_call(
        functools.partial(_gdn_prompt_kernel, cs=math.gcd(tile, GDN_CHUNK)), grid=(batch, steps),
        in_specs=[row(GDN_CONV_DIM), row(LANES), pl.BlockSpec((SMALL_COLS, tile), lambda b, t: (0, b * steps + t)),
                  row(GDN_WIDTH), full(cw), full(alog_row), full(dtb_row), full(alog_col), full(dtb_col), full(nw)],
        out_specs=(row(GDN_WIDTH), pl.BlockSpec((1, GDN_HEADS, hd, hd), lambda b, t: (b, 0, 0, 0))),
        out_shape=(jax.ShapeDtypeStruct((m, GDN_WIDTH), F32),
                   jax.ShapeDtypeStruct((batch, GDN_HEADS, hd, hd), F32)),
        scratch_shapes=[pltpu.VMEM((tile + SUBLANES, GDN_CONV_DIM), F32), pltpu.VMEM((GDN_HEADS, hd, hd), F32)],
        compiler_params=_params("arbitrary", "arbitrary"), name="gdn_prompt")(
            qkv, sm, smt, gg, cw, alog_row, dtb_row, alog_col, dtb_col, nw)


def _gdn_sample_kernel(qkv_ref, hist_ref, sm_ref, smt_ref, gg_ref, s0_ref, cw_ref, alog_row, dtb_row,
                       alog_col, dtb_col, nw_ref, go_ref, s_out_ref, xbuf_ref, *, ls):
    bt = qkv_ref.shape[0]
    xbuf_ref[:, 0:SUBLANES, :] = hist_ref[...]
    xbuf_ref[:, SUBLANES:, :] = qkv_ref[...]

    def conv(cols):
        acc = xbuf_ref[:, SUBLANES:SUBLANES + ls, cols] * cw_ref[CONV_WIDTH - 1:CONV_WIDTH, cols]
        for w in range(CONV_WIDTH - 1):
            lo = SUBLANES - (CONV_WIDTH - 1) + w
            acc = acc + xbuf_ref[:, lo:lo + ls, cols] * cw_ref[w:w + 1, cols]
        return acc.reshape(bt * ls, acc.shape[-1])

    sm = sm_ref[...].reshape(bt * ls, LANES)
    smt = smt_ref[...]
    for h in range(GDN_HEADS):
        u_base, k_cum, qk, q_dec, k_dec, gl = _gdn_tile(
            h, conv, sm, smt, gg_ref, cw_ref, alog_row[...], dtb_row[...], alog_col[...], dtb_col[...], nw_ref, ls)
        us, ois = [], []
        for bi in range(bt):
            rows = slice(bi * ls, (bi + 1) * ls)
            state = s0_ref[bi, h]
            u = u_base[rows] - _dot(k_cum[rows], state)
            ois.append(_dot(q_dec[rows], state))
            s_out_ref[bi, h] = state * jnp.exp(gl[bi * ls:bi * ls + 1, :]) + _dot_tn(k_dec[rows], u)
            us.append(u)
        o = jnp.concatenate(ois, axis=0) + _dot(qk.astype(BF16), jnp.concatenate(us, axis=0).astype(BF16))
        hd = GDN_HEAD_DIM
        cols = slice(h * hd, (h + 1) * hd)
        gate = _silu(gg_ref[:, :, cols].reshape(bt * ls, hd))
        go_ref[:, :, cols] = (_rms(o, nw_ref[...]) * gate).reshape(bt, ls, hd)


def _gdn_sample(qkv, hist, sm, smt, gg, s0, cw, alog_row, dtb_row, alog_col, dtb_col, nw, bt):
    bs, ls, _ = qkv.shape
    per = lambda n: pl.BlockSpec((bt, ls, n), lambda i: (i, 0, 0))
    full = lambda a: pl.BlockSpec(a.shape, lambda i: (0,) * a.ndim)
    hd = GDN_HEAD_DIM
    st = pl.BlockSpec((bt, GDN_HEADS, hd, hd), lambda i: (i, 0, 0, 0))
    return pl.pallas_call(
        functools.partial(_gdn_sample_kernel, ls=ls), grid=(bs // bt,),
        in_specs=[per(GDN_CONV_DIM), pl.BlockSpec((bt, SUBLANES, GDN_CONV_DIM), lambda i: (i, 0, 0)), per(LANES),
                  pl.BlockSpec((SMALL_COLS, bt * ls), lambda i: (0, i)), per(GDN_WIDTH), st,
                  full(cw), full(alog_row), full(dtb_row), full(alog_col), full(dtb_col), full(nw)],
        out_specs=(per(GDN_WIDTH), st),
        out_shape=(jax.ShapeDtypeStruct((bs, ls, GDN_WIDTH), F32), jax.ShapeDtypeStruct(s0.shape, F32)),
        scratch_shapes=[pltpu.VMEM((bt, SUBLANES + ls, GDN_CONV_DIM), F32)],
        compiler_params=_params("arbitrary"), name="gdn_sample")(
            qkv, hist, sm, smt, gg, s0, cw, alog_row, dtb_row, alog_col, dtb_col, nw)


def _norm_matmul_kernel(x_ref, g_ref, w_ref, *o_refs):
    xn = _rms(x_ref[...], g_ref[...]).astype(BF16)
    n = w_ref.shape[1] // len(o_refs)
    for i, o_ref in enumerate(o_refs):
        o_ref[...] = _dot(xn, w_ref[:, i * n:(i + 1) * n])


def _norm_matmul(x, g, w, n_out, tm):
    m = x.shape[0]
    n = w.shape[1] // n_out
    return pl.pallas_call(
        _norm_matmul_kernel, grid=(m // tm,),
        in_specs=[pl.BlockSpec((tm, x.shape[1]), lambda i: (i, 0)), pl.BlockSpec(g.shape, lambda i: (0, 0)),
                  pl.BlockSpec(w.shape, lambda i: (0, 0))],
        out_specs=tuple(pl.BlockSpec((tm, n), lambda i: (i, 0)) for _ in range(n_out)),
        out_shape=tuple(jax.ShapeDtypeStruct((m, n), F32) for _ in range(n_out)),
        compiler_params=_params("arbitrary"), name="norm_matmul")(x, g, w)


def _matmul_residual_kernel(*refs, n_in):
    x_ref, o_ref = refs[0], refs[-1]
    acc = x_ref[...]
    for a_ref, w_ref in zip(refs[1:1 + n_in], refs[1 + n_in:1 + 2 * n_in]):
        acc = acc + _dot(a_ref[...].astype(BF16), w_ref[...])
    o_ref[...] = acc


def _matmul_residual(x, a_list, w_list, tm):
    m, d = x.shape
    n_in = len(a_list)
    in_specs = [pl.BlockSpec((tm, d), lambda i: (i, 0))]
    in_specs += [pl.BlockSpec((tm, a.shape[1]), lambda i: (i, 0)) for a in a_list]
    in_specs += [pl.BlockSpec(w.shape, lambda i: (0, 0)) for w in w_list]
    return pl.pallas_call(
        functools.partial(_matmul_residual_kernel, n_in=n_in), grid=(m // tm,), in_specs=in_specs,
        out_specs=pl.BlockSpec((tm, d), lambda i: (i, 0)), out_shape=jax.ShapeDtypeStruct((m, d), F32),
        compiler_params=_params("arbitrary"), name="matmul_residual")(x, *a_list, *w_list)


def _mem_attn_prompt_kernel(x_ref, g_ref, wq_ref, wo_ref, mk_ref, mv_ref, o_ref):
    x = x_ref[0]
    xn = _rms(x, g_ref[...]).astype(BF16)
    q = _dot(xn, wq_ref[...]).astype(BF16)
    outs = []
    for h in range(MEM_HEADS):
        cols = slice(h * MEM_HEAD_DIM, (h + 1) * MEM_HEAD_DIM)
        s = _dot_nt(q[:, cols], mk_ref[0, :, cols].astype(BF16)) * (MEM_HEAD_DIM ** -0.5)
        p = jnp.exp(s - jnp.max(s, axis=1, keepdims=True))
        o = _dot(p.astype(BF16), mv_ref[0, :, cols].astype(BF16))
        outs.append(o / jnp.sum(p, axis=1, keepdims=True))
    o_ref[0] = x + _dot(jnp.concatenate(outs, axis=1).astype(BF16), wo_ref[...])


def _mem_attn_prompt(x, g, wq, wo, mk, mv, tm):
    b, l, d = x.shape
    n_mem = mk.shape[1]
    full = lambda a: pl.BlockSpec(a.shape, lambda bi, i: (0,) * a.ndim)
    mem_spec = pl.BlockSpec((1, n_mem, MEM_WIDTH), lambda bi, i: (bi, 0, 0))
    x_spec = pl.BlockSpec((1, tm, d), lambda bi, i: (bi, i, 0))
    return pl.pallas_call(
        _mem_attn_prompt_kernel, grid=(b, l // tm),
        in_specs=[x_spec, full(g), full(wq), full(wo), mem_spec, mem_spec],
        out_specs=x_spec, out_shape=jax.ShapeDtypeStruct(x.shape, F32),
        compiler_params=_params("arbitrary", "arbitrary"), name="mem_attn_prompt")(x, g, wq, wo, mk, mv)


def _mem_attn_sample_kernel(q_ref, mk_ref, mv_ref, o_ref):
    bt, ls, _ = q_ref.shape
    rows = MEM_HEADS * ls
    headmask = _div(_iota((rows, MEM_WIDTH), 0), ls) == _div(_iota((rows, MEM_WIDTH), 1), MEM_HEAD_DIM)
    for bi in range(bt):
        qbd = jnp.where(headmask, jnp.concatenate([q_ref[bi]] * MEM_HEADS, axis=0), 0.0).astype(BF16)
        s = _dot_nt(qbd, mk_ref[bi].astype(BF16)) * (MEM_HEAD_DIM ** -0.5)
        p = jnp.exp(s - jnp.max(s, axis=1, keepdims=True))
        o = _dot(p.astype(BF16), mv_ref[bi].astype(BF16)) / jnp.sum(p, axis=1, keepdims=True)
        o_ref[bi] = jnp.sum(jnp.where(headmask, o, 0.0).reshape(MEM_HEADS, ls, MEM_WIDTH), axis=0)


def _mem_attn_sample(q, mk, mv, bt):
    bs, ls, _ = q.shape
    n_mem = mk.shape[1]
    mem_spec = pl.BlockSpec((bt, n_mem, MEM_WIDTH), lambda i: (i, 0, 0))
    q_spec = pl.BlockSpec((bt, ls, MEM_WIDTH), lambda i: (i, 0, 0))
    return pl.pallas_call(
        _mem_attn_sample_kernel, grid=(bs // bt,), in_specs=[q_spec, mem_spec, mem_spec],
        out_specs=q_spec, out_shape=jax.ShapeDtypeStruct(q.shape, F32),
        compiler_params=_params("arbitrary"), name="mem_attn_sample")(q, mk, mv)


def _ffn_kernel(x_ref, g_ref, wa_ref, wu_ref, wo_ref, gf_ref, o_ref, *, chunk, final_norm):
    x = x_ref[...]
    xn = _rms(x, g_ref[...]).astype(BF16)
    acc = x
    for c in range(wa_ref.shape[1] // chunk):
        cols = slice(c * chunk, (c + 1) * chunk)
        a = _dot(xn, wa_ref[:, cols])
        u = _dot(xn, wu_ref[:, cols])
        acc = acc + _dot((_silu(a) * u).astype(BF16), wo_ref[cols, :])
    if final_norm:
        acc = _rms(acc, gf_ref[...])
    o_ref[...] = acc


def _ffn(x, g, wa, wu, wo, gf, final_norm, tm):
    m, d = x.shape
    full = lambda a: pl.BlockSpec(a.shape, lambda i: (0,) * a.ndim)
    row = pl.BlockSpec((tm, d), lambda i: (i, 0))
    return pl.pallas_call(
        functools.partial(_ffn_kernel, chunk=2 * LANES, final_norm=final_norm), grid=(m // tm,),
        in_specs=[row, full(g), full(wa), full(wu), full(wo), full(gf)], out_specs=row,
        out_shape=jax.ShapeDtypeStruct((m, d), F32), compiler_params=_params("arbitrary"), name="ffn")(
            x, g, wa, wu, wo, gf)


def _pick(m, cap):
    t = min(m, cap)
    while m % t:
        t -= SUBLANES
    assert t > 0 and m % t == 0, (m, cap)
    return t


def _pad_lanes(v, offset):
    row = jnp.zeros((1, LANES), F32).at[0, offset:offset + v.shape[0]].set(v)
    return row, row[0, :SMALL_COLS].reshape(SMALL_COLS, 1)


def kernel(x_prompt, x_sample, cache_fox_k, cache_fox_v, cache_fox_logf, state_gdn, state_gdn_conv, cache_mem_k, cache_mem_v, page_table, mem_prompt, g_norm_mix, w_in, b_fox_f, gdn_conv_w, gdn_a_log, gdn_dt_bias, gdn_norm_w, w_out, g_norm_memin, w_mem_kv, g_norm_mem, w_mem_q, w_mem_o, g_norm_ffn, w_ffn_in, w_ffn_out, g_final):
    bp, lp, d = x_prompt.shape
    bs, ls, _ = x_sample.shape
    depth = w_in.shape[0]
    n_phys, page = cache_fox_k.shape[1], cache_fox_k.shape[2]
    n_pages = page_table.shape[1]
    n_mem = mem_prompt.shape[1]
    d_ff = w_ffn_out.shape[1]
    mp, ms = bp * lp, bs * ls
    hd = GDN_HEAD_DIM

    yp = x_prompt.reshape(mp, d)
    ys = x_sample.reshape(ms, d)
    mem_flat = mem_prompt.reshape(bp * n_mem, d)
    cache_k = cache_fox_k.transpose(0, 1, 3, 4, 2).reshape(depth, n_phys, FOX_WIDTH, page)
    cache_v = cache_fox_v.transpose(0, 1, 3, 4, 2).reshape(depth, n_phys, FOX_WIDTH, page)
    page_rows = depth * n_phys * FOX_HEADS
    cache_cum = _cumsum_pages(cache_fox_logf.transpose(0, 1, 3, 2).reshape(page_rows, page), _pick(page_rows, 512))
    cache_cum = cache_cum.reshape(depth, n_phys, FOX_HEADS, page)
    tokens_per_row = LANES // FOX_HEADS
    hist = jnp.pad(state_gdn_conv, ((0, 0), (0, 0), (SUBLANES - (CONV_WIDTH - 1), 0), (0, 0)))
    mem_k_cache = cache_mem_k.reshape(depth, bs, n_mem, MEM_WIDTH)
    mem_v_cache = cache_mem_v.reshape(depth, bs, n_mem, MEM_WIDTH)

    o1 = FOX_WIDTH * 3
    o2 = o1 + FOX_HEADS
    o3 = o2 + GDN_CONV_DIM
    o4 = o3 + 2 * GDN_HEADS
    row2 = lambda v: v.reshape(1, -1)

    outs_p = {k: [] for k in ("lf", "gs", "gc", "mk", "mv")}
    outs_s = {k: [] for k in ("fk", "fv", "lf", "gs", "gc")}
    kt_all = vt_all = None
    for l in range(depth):
        w = w_in[l]
        wq = w[:, :FOX_WIDTH].astype(BF16)
        wkv = w[:, FOX_WIDTH:o1].astype(BF16)
        wkvt = w[:, FOX_WIDTH:o1].T.astype(BF16)
        wg = jnp.concatenate([w[:, o2:o3], w[:, o4:]], axis=1).astype(BF16)
        w_small = jnp.concatenate([w[:, o1:o2], w[:, o3:o4]], axis=1)
        ws = jnp.pad(w_small, ((0, 0), (0, LANES - SMALL_COLS))).astype(BF16)
        wst = w_small.T.astype(BF16)
        b_row = row2(b_fox_f[l])
        b_col = b_fox_f[l].reshape(FOX_HEADS, 1)
        b_pad = jnp.pad(b_row, ((0, 0), (0, LANES - FOX_HEADS)))
        alog_row, alog_col = _pad_lanes(gdn_a_log[l], GA_COL)
        dtb_row, dtb_col = _pad_lanes(gdn_dt_bias[l], GA_COL)
        cw = gdn_conv_w[l]
        nw = row2(gdn_norm_w[l])
        wo_fox = w_out[l][:FOX_WIDTH].astype(BF16)
        wo_gdn = w_out[l][FOX_WIDTH:].astype(BF16)
        w_kv = w_mem_kv[l].astype(BF16)
        w_q = w_mem_q[l].astype(BF16)
        w_o = w_mem_o[l].astype(BF16)
        wa = w_ffn_in[l][:, :d_ff].astype(BF16)
        wu = w_ffn_in[l][:, d_ff:].astype(BF16)
        wf = w_ffn_out[l].astype(BF16)
        g_mix, g_memin, g_mem, g_ffn = row2(g_norm_mix[l]), row2(g_norm_memin[l]), row2(g_norm_mem[l]), row2(g_norm_ffn[l])
        gf = row2(g_final)
        last = l == depth - 1

        fqb, kt_all, vt_all, ktb, vtb, qkv, gg, lf, lft, sm, smt = _in_proj_prompt(
            l, depth, bp, yp, g_mix, wq, wkvt, wg, ws, wst, b_row, b_col, kt_all, vt_all, _pick(lp, 256))
        seg = lp // tokens_per_row
        c_col = _cumsum_tokens(lf.reshape(mp // tokens_per_row, LANES), seg, seg).reshape(bp, lp, FOX_HEADS)
        c_row = _cumsum_lanes(lft, _pick(lp, 512))
        fo = _fox_prompt(fqb.reshape(bp, lp, FOX_WIDTH), ktb, vtb, c_col, c_row, _pick(lp, 512)).reshape(mp, FOX_WIDTH)
        go, s_new = _gdn_prompt(qkv, sm, smt, gg, cw, alog_row, dtb_row, alog_col, dtb_col, nw, bp, _pick(lp, 256))
        yp = _matmul_residual(yp, [fo, go], [wo_fox, wo_gdn], _pick(mp, 512))
        mk, mv = _norm_matmul(mem_flat, g_memin, w_kv, 2, _pick(bp * n_mem, 256))
        yp = _mem_attn_prompt(yp.reshape(bp, lp, d), g_mem, w_q, w_o, mk.reshape(bp, n_mem, MEM_WIDTH),
                              mv.reshape(bp, n_mem, MEM_WIDTH), _pick(lp, 512)).reshape(mp, d)
        yp = _ffn(yp, g_ffn, wa, wu, wf, gf, last, _pick(mp, 256))
        outs_p["lf"].append(lft.transpose(0, 2, 1))
        outs_p["gs"].append(s_new)
        outs_p["gc"].append(qkv.reshape(bp, lp, GDN_CONV_DIM)[:, lp - (CONV_WIDTH - 1):])
        outs_p["mk"].append(mk.reshape(bp, n_mem, MEM_HEADS, MEM_HEAD_DIM))
        outs_p["mv"].append(mv.reshape(bp, n_mem, MEM_HEADS, MEM_HEAD_DIM))

        fqb, fk, fv, qkv, gg, lf, sm, smt = _in_proj_sample(ys, g_mix, wq, wkv, wg, ws, wst, b_row, _pick(ms, 256))
        s3 = lambda a: a.reshape(bs, ls, a.shape[-1])
        fo = _fox_decode(l, page_table, s3(fqb), s3(fk), s3(fv), s3(sm), b_pad, cache_k, cache_v, cache_cum,
                         n_pages // 2).reshape(ms, FOX_WIDTH)
        go, s_new = _gdn_sample(s3(qkv), hist[l], s3(sm), smt, s3(gg), state_gdn[l], cw, alog_row, dtb_row,
                                alog_col, dtb_col, nw, _pick(bs, 16))
        ys = _matmul_residual(ys, [fo, go.reshape(ms, GDN_WIDTH)], [wo_fox, wo_gdn], _pick(ms, 512))
        (q_mem,) = _norm_matmul(ys, g_mem, w_q, 1, _pick(ms, 256))
        o_mem = _mem_attn_sample(s3(q_mem), mem_k_cache[l], mem_v_cache[l], _pick(bs, 8)).reshape(ms, MEM_WIDTH)
        ys = _matmul_residual(ys, [o_mem], [w_o], _pick(ms, 512))
        ys = _ffn(ys, g_ffn, wa, wu, wf, gf, last, _pick(ms, 256))
        outs_s["fk"].append(fk.reshape(bs, ls, FOX_HEADS, FOX_HEAD_DIM))
        outs_s["fv"].append(fv.reshape(bs, ls, FOX_HEADS, FOX_HEAD_DIM))
        outs_s["lf"].append(lf.reshape(bs, ls, FOX_HEADS))
        outs_s["gs"].append(s_new)
        outs_s["gc"].append(s3(qkv)[:, ls - (CONV_WIDTH - 1):])

    st = jnp.stack
    heads_last = lambda a: a.reshape(depth, bp, FOX_HEADS, FOX_HEAD_DIM, lp).transpose(0, 1, 4, 2, 3)
    return (yp.reshape(bp, lp, d), ys.reshape(bs, ls, d),
            heads_last(kt_all), heads_last(vt_all), st(outs_p["lf"]), st(outs_p["gs"]), st(outs_p["gc"]),
            st(outs_p["mk"]), st(outs_p["mv"]),
            st(outs_s["fk"]), st(outs_s["fv"]), st(outs_s["lf"]), st(outs_s["gs"]), st(outs_s["gc"]))
```

```python
import functools
import math

import jax
import jax.numpy as jnp
from jax import lax
from jax.experimental import pallas as pl
from jax.experimental.pallas import tpu as pltpu

F32 = jnp.float32
BF16 = jnp.bfloat16
HI = lax.Precision.HIGHEST

EPS = 1e-6
D_MODEL = 1024
FOX_HEADS = 8
FOX_HEAD_DIM = 64
FOX_WIDTH = FOX_HEADS * FOX_HEAD_DIM
GDN_HEADS = 4
GDN_HEAD_DIM = 128
GDN_WIDTH = GDN_HEADS * GDN_HEAD_DIM
GDN_CONV_DIM = 3 * GDN_WIDTH
CONV_WIDTH = 4
GDN_CHUNK = 64
MEM_HEADS = 4
MEM_HEAD_DIM = 128
MEM_WIDTH = MEM_HEADS * MEM_HEAD_DIM
LANES = 128
SUBLANES = 8
SMALL_COLS = 16
GA_COL = FOX_HEADS
GB_COL = FOX_HEADS + GDN_HEADS
VMEM_LIMIT = 56 * 1024 * 1024


def _params(*sem):
    return pltpu.CompilerParams(dimension_semantics=sem, vmem_limit_bytes=VMEM_LIMIT)


def _dot(a, b):
    return jnp.dot(a, b, preferred_element_type=F32)


def _dot_nt(a, b):
    return lax.dot_general(a, b, (((1,), (1,)), ((), ())), preferred_element_type=F32)


def _dot_tn(a, b):
    return lax.dot_general(a, b, (((0,), (0,)), ((), ())), preferred_element_type=F32)


def _dot_hi(a, b):
    return jnp.dot(a, b, precision=HI, preferred_element_type=F32)


def _dot_nt_hi(a, b):
    return lax.dot_general(a, b, (((1,), (1,)), ((), ())), precision=HI, preferred_element_type=F32)


def _rms(x, g):
    return x * lax.rsqrt(jnp.mean(x * x, axis=-1, keepdims=True) + EPS) * g


def _log_sigmoid(x):
    return jnp.minimum(x, 0.0) - jnp.log1p(jnp.exp(-jnp.abs(x)))


def _softplus(x):
    return jnp.maximum(x, 0.0) + jnp.log1p(jnp.exp(-jnp.abs(x)))


def _silu(x):
    return x * jax.nn.sigmoid(x)


def _iota(shape, dim):
    return lax.broadcasted_iota(jnp.int32, shape, dim)


def _log2(n):
    lg = n.bit_length() - 1
    assert n == 1 << lg, n
    return lg


def _div(x, n):
    return x >> _log2(n)


def _mod(x, n):
    _log2(n)
    return x & (n - 1)


def _in_proj_common(x_ref, g_ref, wq_ref, wg_ref, ws_ref, wst_ref, b_ref,
                    fqb_ref, qkv_ref, gg_ref, lf_ref, sm_ref, smt_ref):
    xn = _rms(x_ref[...], g_ref[...]).astype(BF16)
    fqb_ref[...] = (_dot(xn, wq_ref[...]) * (FOX_HEAD_DIM ** -0.5)).astype(BF16)
    qkv_ref[...] = _dot(xn, wg_ref[:, 0:GDN_CONV_DIM])
    gg_ref[...] = _dot(xn, wg_ref[:, GDN_CONV_DIM:])
    sm = _dot(xn, ws_ref[...])
    smt = _dot_nt(wst_ref[...], xn)
    sm_ref[...] = sm
    smt_ref[...] = smt
    lf_ref[...] = _log_sigmoid(sm[:, 0:FOX_HEADS] + b_ref[...])
    return xn, smt


def _in_proj_prompt_kernel(x_ref, g_ref, wq_ref, wkvt_ref, wg_ref, ws_ref, wst_ref, b_ref, bcol_ref, *rest):
    (fqb_ref, kt_ref, vt_ref, ktb_ref, vtb_ref, qkv_ref, gg_ref, lf_ref, lft_ref, sm_ref, smt_ref) = rest[-11:]
    xn, smt = _in_proj_common(x_ref, g_ref, wq_ref, wg_ref, ws_ref, wst_ref, b_ref,
                              fqb_ref, qkv_ref, gg_ref, lf_ref, sm_ref, smt_ref)
    kt = _dot_nt(wkvt_ref[0:FOX_WIDTH, :], xn)
    kt_ref[0, 0] = kt
    ktb_ref[0] = kt.astype(BF16)
    vt = _dot_nt(wkvt_ref[FOX_WIDTH:, :], xn)
    vt_ref[0, 0] = vt
    vtb_ref[0] = vt.astype(BF16)
    lft_ref[0] = _log_sigmoid(smt[0:FOX_HEADS, :] + bcol_ref[...])


def _in_proj_prompt(layer, depth, batch, x, g, wq, wkvt, wg, ws, wst, b_row, b_col, prev_kt, prev_vt, tm):
    m = x.shape[0]
    l = m // batch
    nt = l // tm
    row = lambda n: pl.BlockSpec((tm, n), lambda i: (i, 0))
    full = lambda a: pl.BlockSpec(a.shape, lambda i: (0,) * a.ndim)
    per_b = lambda rows: pl.BlockSpec((1, rows, tm), lambda i: (i // nt, 0, i % nt))
    kv_all = pl.BlockSpec((1, 1, FOX_WIDTH, tm), lambda i: (layer, i // nt, 0, i % nt))
    out_shape = (
        jax.ShapeDtypeStruct((m, FOX_WIDTH), BF16),
        jax.ShapeDtypeStruct((depth, batch, FOX_WIDTH, l), F32),
        jax.ShapeDtypeStruct((depth, batch, FOX_WIDTH, l), F32),
        jax.ShapeDtypeStruct((batch, FOX_WIDTH, l), BF16),
        jax.ShapeDtypeStruct((batch, FOX_WIDTH, l), BF16),
        jax.ShapeDtypeStruct((m, GDN_CONV_DIM), F32),
        jax.ShapeDtypeStruct((m, GDN_WIDTH), F32),
        jax.ShapeDtypeStruct((m, FOX_HEADS), F32),
        jax.ShapeDtypeStruct((batch, FOX_HEADS, l), F32),
        jax.ShapeDtypeStruct((m, LANES), F32),
        jax.ShapeDtypeStruct((SMALL_COLS, m), F32),
    )
    out_specs = (row(FOX_WIDTH), kv_all, kv_all, per_b(FOX_WIDTH), per_b(FOX_WIDTH),
                 row(GDN_CONV_DIM), row(GDN_WIDTH), row(FOX_HEADS), per_b(FOX_HEADS), row(LANES),
                 pl.BlockSpec((SMALL_COLS, tm), lambda i: (0, i)))
    args = [x, g, wq, wkvt, wg, ws, wst, b_row, b_col]
    in_specs = [row(D_MODEL)] + [full(a) for a in args[1:]]
    aliases = {}
    if prev_kt is not None:
        aliases = {len(args): 1, len(args) + 1: 2}
        args += [prev_kt, prev_vt]
        in_specs += [pl.BlockSpec(memory_space=pl.ANY)] * 2
    return pl.pallas_call(
        _in_proj_prompt_kernel, grid=(m // tm,), in_specs=in_specs, out_specs=out_specs, out_shape=out_shape,
        input_output_aliases=aliases, compiler_params=_params("arbitrary"), name="in_proj_prompt")(*args)


def _in_proj_sample_kernel(x_ref, g_ref, wq_ref, wkv_ref, wg_ref, ws_ref, wst_ref, b_ref,
                           fqb_ref, fk_ref, fv_ref, qkv_ref, gg_ref, lf_ref, sm_ref, smt_ref):
    xn, _ = _in_proj_common(x_ref, g_ref, wq_ref, wg_ref, ws_ref, wst_ref, b_ref,
                            fqb_ref, qkv_ref, gg_ref, lf_ref, sm_ref, smt_ref)
    fk_ref[...] = _dot(xn, wkv_ref[:, 0:FOX_WIDTH])
    fv_ref[...] = _dot(xn, wkv_ref[:, FOX_WIDTH:])


def _in_proj_sample(x, g, wq, wkv, wg, ws, wst, b_row, tm):
    m = x.shape[0]
    row = lambda n: pl.BlockSpec((tm, n), lambda i: (i, 0))
    full = lambda a: pl.BlockSpec(a.shape, lambda i: (0,) * a.ndim)
    out_shape = (
        jax.ShapeDtypeStruct((m, FOX_WIDTH), BF16),
        jax.ShapeDtypeStruct((m, FOX_WIDTH), F32),
        jax.ShapeDtypeStruct((m, FOX_WIDTH), F32),
        jax.ShapeDtypeStruct((m, GDN_CONV_DIM), F32),
        jax.ShapeDtypeStruct((m, GDN_WIDTH), F32),
        jax.ShapeDtypeStruct((m, FOX_HEADS), F32),
        jax.ShapeDtypeStruct((m, LANES), F32),
        jax.ShapeDtypeStruct((SMALL_COLS, m), F32),
    )
    out_specs = (row(FOX_WIDTH), row(FOX_WIDTH), row(FOX_WIDTH), row(GDN_CONV_DIM), row(GDN_WIDTH),
                 row(FOX_HEADS), row(LANES), pl.BlockSpec((SMALL_COLS, tm), lambda i: (0, i)))
    args = [x, g, wq, wkv, wg, ws, wst, b_row]
    return pl.pallas_call(
        _in_proj_sample_kernel, grid=(m // tm,), in_specs=[row(D_MODEL)] + [full(a) for a in args[1:]],
        out_specs=out_specs, out_shape=out_shape, compiler_params=_params("arbitrary"),
        name="in_proj_sample")(*args)


def _cumsum_tokens_kernel(x_ref, o_ref, *, seg_rows):
    x = x_ref[...]
    rows = x.shape[0]
    a = _iota((LANES, LANES), 0)
    b = _iota((LANES, LANES), 1)
    same_head = (a & (FOX_HEADS - 1)) == (b & (FOX_HEADS - 1))
    upto = (same_head & ((a >> 3) <= (b >> 3))).astype(F32)
    every = same_head.astype(F32)
    r = _iota((rows, rows), 0)
    c = _iota((rows, rows), 1)
    sh = _log2(seg_rows)
    before = ((c < r) & ((c >> sh) == (r >> sh))).astype(F32)
    o_ref[...] = _dot_hi(x, upto) + _dot_hi(before, _dot_hi(x, every))


def _cumsum_tokens(x, seg_rows, block_rows):
    rows = x.shape[0]
    spec = pl.BlockSpec((block_rows, LANES), lambda i: (i, 0))
    return pl.pallas_call(
        functools.partial(_cumsum_tokens_kernel, seg_rows=seg_rows), grid=(rows // block_rows,),
        in_specs=[spec], out_specs=spec, out_shape=jax.ShapeDtypeStruct(x.shape, F32),
        compiler_params=_params("arbitrary"), name="cumsum_tokens")(x)


def _cumsum_lanes_kernel(x_ref, o_ref, carry_ref):
    @pl.when(pl.program_id(1) == 0)
    def _():
        carry_ref[...] = jnp.zeros_like(carry_ref)
    x = x_ref[0]
    n = x.shape[1]
    upto = (_iota((n, n), 0) <= _iota((n, n), 1)).astype(F32)
    y = _dot_hi(x, upto) + carry_ref[...]
    o_ref[0] = y
    carry_ref[...] = y[:, n - 1:n]


def _cumsum_lanes(xt, chunk):
    batch, heads, l = xt.shape
    spec = pl.BlockSpec((1, heads, chunk), lambda b, j: (b, 0, j))
    return pl.pallas_call(
        _cumsum_lanes_kernel, grid=(batch, l // chunk), in_specs=[spec], out_specs=spec,
        out_shape=jax.ShapeDtypeStruct(xt.shape, F32),
        scratch_shapes=[pltpu.VMEM((heads, 1), F32)],
        compiler_params=_params("arbitrary", "arbitrary"), name="cumsum_lanes")(xt)


def _cumsum_pages_kernel(x_ref, o_ref):
    n = x_ref.shape[1]
    upto = (_iota((n, n), 0) <= _iota((n, n), 1)).astype(F32)
    o_ref[...] = _dot_hi(x_ref[...], upto)


def _cumsum_pages(x, block_rows):
    rows, n = x.shape
    spec = pl.BlockSpec((block_rows, n), lambda i: (i, 0))
    return pl.pallas_call(
        _cumsum_pages_kernel, grid=(rows // block_rows,), in_specs=[spec], out_specs=spec,
        out_shape=jax.ShapeDtypeStruct(x.shape, F32), compiler_params=_params("arbitrary"),
        name="cumsum_pages")(x)


def _fox_prompt_kernel(q_ref, kt_ref, vt_ref, cq_ref, ck_ref, o_ref, m_ref, l_ref, acc_ref, *, tq, tk):
    qi = pl.program_id(1)
    ki = pl.program_id(2)

    @pl.when(ki == 0)
    def _():
        m_ref[...] = jnp.full_like(m_ref, -jnp.inf)
        l_ref[...] = jnp.zeros_like(l_ref)
        acc_ref[...] = jnp.zeros_like(acc_ref)

    low = _iota((tq, LANES), 1) < FOX_HEAD_DIM

    def step(diagonal):
        if diagonal:
            visible = _iota((tq, tk), 1) <= _iota((tq, tk), 0)
        for j in range(FOX_HEADS // 2):
            cols = slice(j * LANES, (j + 1) * LANES)
            qp = q_ref[0, :, cols]
            kp = kt_ref[0, cols, :]
            vp = vt_ref[0, cols, :]
            alphas, outs = [], []
            for half in range(2):
                h = 2 * j + half
                qh = jnp.where(low if half == 0 else ~low, qp, jnp.zeros_like(qp))
                t = _dot(qh, kp) - ck_ref[0, h:h + 1, :]
                if diagonal:
                    t = jnp.where(visible, t, -jnp.inf)
                cq = cq_ref[0, :, h:h + 1]
                m_old = m_ref[h]
                m_new = jnp.maximum(m_old, jnp.max(t, axis=1, keepdims=True) + cq)
                alpha = jnp.exp(m_old - m_new)
                p = jnp.exp(t + (cq - m_new))
                l_ref[h] = alpha * l_ref[h] + jnp.sum(p, axis=1, keepdims=True)
                m_ref[h] = m_new
                alphas.append(alpha)
                outs.append(_dot_nt(p.astype(BF16), vp))
            acc_ref[j] = (jnp.where(low, alphas[0], alphas[1]) * acc_ref[j]
                          + jnp.where(low, outs[0], outs[1]))

    @pl.when(ki < qi)
    def _():
        step(False)

    @pl.when(ki == qi)
    def _():
        step(True)
        for j in range(FOX_HEADS // 2):
            inv = jnp.where(low, 1.0 / l_ref[2 * j], 1.0 / l_ref[2 * j + 1])
            o_ref[0, :, j * LANES:(j + 1) * LANES] = acc_ref[j] * inv


def _fox_prompt(qb, ktb, vtb, c_col, c_row, tq):
    b, l, _ = qb.shape
    tk = tq
    n = l // tq
    kv_spec = pl.BlockSpec((1, FOX_WIDTH, tk), lambda bi, qi, ki: (bi, 0, jnp.minimum(ki, qi)))
    return pl.pallas_call(
        functools.partial(_fox_prompt_kernel, tq=tq, tk=tk), grid=(b, n, n),
        in_specs=[pl.BlockSpec((1, tq, FOX_WIDTH), lambda bi, qi, ki: (bi, qi, 0)), kv_spec, kv_spec,
                  pl.BlockSpec((1, tq, FOX_HEADS), lambda bi, qi, ki: (bi, qi, 0)),
                  pl.BlockSpec((1, FOX_HEADS, tk), lambda bi, qi, ki: (bi, 0, jnp.minimum(ki, qi)))],
        out_specs=pl.BlockSpec((1, tq, FOX_WIDTH), lambda bi, qi, ki: (bi, qi, 0)),
        out_shape=jax.ShapeDtypeStruct((b, l, FOX_WIDTH), F32),
        scratch_shapes=[pltpu.VMEM((FOX_HEADS, tq, 1), F32), pltpu.VMEM((FOX_HEADS, tq, 1), F32),
                        pltpu.VMEM((FOX_HEADS // 2, tq, LANES), F32)],
        compiler_params=_params("arbitrary", "arbitrary", "arbitrary"), name="fox_prompt")(qb, ktb, vtb, c_col, c_row)


def _fox_decode_kernel(pt_ref, q_ref, kn_ref, vn_ref, sm_ref, b_ref, *rest, n_pages, page, ls):
    del pt_ref
    k_refs, v_refs, c_refs = rest[0:n_pages], rest[n_pages:2 * n_pages], rest[2 * n_pages:3 * n_pages]
    o_ref = rest[3 * n_pages]
    rows = FOX_HEADS * ls
    headmask = _div(_iota((rows, FOX_WIDTH), 0), ls) == _div(_iota((rows, FOX_WIDTH), 1), FOX_HEAD_DIM)
    expand = (_div(_iota((rows, FOX_HEADS), 0), ls) == _iota((rows, FOX_HEADS), 1)).astype(F32)
    q = q_ref[0].astype(F32)
    qbd = jnp.where(headmask, jnp.concatenate([q] * FOX_HEADS, axis=0), 0.0).astype(BF16)

    lane = _iota((ls, LANES), 1)
    lf_new = jnp.where(lane < FOX_HEADS, _log_sigmoid(sm_ref[0] + b_ref[...]), 0.0)
    tok = _iota((ls, LANES), 0)
    cum_new = lf_new
    for s in range(1, ls):
        cum_new = cum_new + jnp.where(tok >= s, pltpu.roll(lf_new, s, 0), 0.0)
    cum_new = cum_new[:, 0:FOX_HEADS]
    query_of_row = (_mod(_iota((rows, ls), 0), ls) == _iota((rows, ls), 1)).astype(F32)
    cq_row = jnp.sum(_dot_hi(query_of_row, cum_new) * expand, axis=1, keepdims=True)

    vis = _iota((rows, ls), 1) <= _mod(_iota((rows, ls), 0), ls)
    tiles = [jnp.where(vis, _dot_nt(qbd, kn_ref[0].astype(BF16)) - _dot_nt_hi(expand, cum_new), -jnp.inf)]
    reach = jnp.zeros((FOX_HEADS, 1), F32)
    for r in range(n_pages):
        cum_page = c_refs[r][0, 0]
        reach = reach + cum_page[:, page - 1:page]
        tiles.append(_dot(qbd, k_refs[r][0, 0].astype(BF16)) - _dot_hi(expand, cum_page - reach))

    m = cq_row + functools.reduce(jnp.maximum, [jnp.max(t, axis=1, keepdims=True) for t in tiles])
    shift = cq_row - m
    p_new = jnp.exp(tiles[0] + shift)
    l = jnp.sum(p_new, axis=1, keepdims=True)
    acc = _dot(p_new.astype(BF16), vn_ref[0].astype(BF16))
    for r in range(n_pages):
        p = jnp.exp(tiles[r + 1] + shift)
        l = l + jnp.sum(p, axis=1, keepdims=True)
        acc = acc + _dot_nt(p.astype(BF16), v_refs[r][0, 0].astype(BF16))
    full = jnp.where(headmask, acc / l, 0.0)
    o_ref[0] = jnp.sum(full.reshape(FOX_HEADS, ls, FOX_WIDTH), axis=0)


def _fox_decode(layer, page_table, qb, k_new, v_new, sm, b_pad, cache_k, cache_v, cache_cum):
    bs, ls, _ = qb.shape
    n_pages = page_table.shape[1]
    page = cache_k.shape[3]

    def page_map(r):
        return lambda b, pt: (layer, pt[b, n_pages - 1 - r], 0, 0)

    per_sample = lambda n: pl.BlockSpec((1, ls, n), lambda b, pt: (b, 0, 0))
    in_specs = [per_sample(FOX_WIDTH), per_sample(FOX_WIDTH), per_sample(FOX_WIDTH), per_sample(LANES),
                pl.BlockSpec((1, LANES), lambda b, pt: (0, 0))]
    in_specs += [pl.BlockSpec((1, 1, FOX_WIDTH, page), page_map(r)) for r in range(n_pages)]
    in_specs += [pl.BlockSpec((1, 1, FOX_WIDTH, page), page_map(r)) for r in range(n_pages)]
    in_specs += [pl.BlockSpec((1, 1, FOX_HEADS, page), page_map(r)) for r in range(n_pages)]
    grid_spec = pltpu.PrefetchScalarGridSpec(
        num_scalar_prefetch=1, grid=(bs,), in_specs=in_specs,
        out_specs=pl.BlockSpec((1, ls, FOX_WIDTH), lambda b, pt: (b, 0, 0)))
    kern = functools.partial(_fox_decode_kernel, n_pages=n_pages, page=page, ls=ls)
    args = [page_table, qb, k_new, v_new, sm, b_pad]
    args += [cache_k] * n_pages + [cache_v] * n_pages + [cache_cum] * n_pages
    return pl.pallas_call(
        kern, grid_spec=grid_spec, out_shape=jax.ShapeDtypeStruct((bs, ls, FOX_WIDTH), F32),
        compiler_params=_params("arbitrary"), name="fox_decode")(*args)


INV_BASE = 8


def _bf(x):
    return x.astype(BF16)


def _mm(a, b):
    return _dot(_bf(a), _bf(b))


def _split3(x):
    hi = _bf(x)
    r1 = x - hi.astype(F32)
    mid = _bf(r1)
    return hi, mid, _bf(r1 - mid.astype(F32))


def _mm_near_f32(a, b):
    ah, al, _ = _split3(a)
    bh, bl, _ = _split3(b)
    return _dot(ah, bh) + (_dot(ah, bl) + _dot(al, bh))


def _gdn_gates(sm, smt, alog_row, dtb_row, alog_col, dtb_col, cs):
    t_rows = sm.shape[0]
    lg = _log2(cs)
    r = _iota((t_rows, t_rows), 0)
    c = _iota((t_rows, t_rows), 1)
    same = (r >> lg) == (c >> lg)
    incl = same & (c <= r)
    strict = same & (c < r)
    base = min(INV_BASE, cs)
    base_blk = strict & ((r >> _log2(base)) == (c >> _log2(base)))
    merges = []
    s = base
    while s < cs:
        ls_ = _log2(s)
        merges.append(((r >> (ls_ + 1)) == (c >> (ls_ + 1))) & (((r >> ls_) & 1) == 1) & (((c >> ls_) & 1) == 0))
        s *= 2
    g_col = -jnp.exp(alog_row) * _softplus(sm + dtb_row)
    g_row = -jnp.exp(alog_col) * _softplus(smt + dtb_col)
    incl_b, same_b, upto_b = _bf(incl), _bf(same), _bf(same & (r <= c))
    gc_col = sum(_dot(incl_b, part) for part in _split3(g_col))
    gl_col = sum(_dot(same_b, part) for part in _split3(g_col))
    gc_row = sum(_dot(part, upto_b) for part in _split3(g_row))
    return dict(incl=incl, strict=strict, eye=(r == c).astype(F32), base_blk=base_blk, merges=merges,
                base=base, gc_col=gc_col, gl_col=gl_col, gc_row=gc_row, beta=jax.nn.sigmoid(sm))


def _each(f, *lists):
    return [f(*args) for args in zip(*lists)]


def _unit_lower_inverses(a_list, gt):
    powers = _each(lambda a: jnp.where(gt["base_blk"], -a, 0.0), a_list)
    invs = _each(lambda n0: gt["eye"] + n0, powers)
    for _ in range(_log2(gt["base"]) - 1):
        powers = _each(lambda p: _mm(p, p), powers)
        invs = _each(lambda inv, p: inv + _mm(inv, p), invs, powers)
    for merge in gt["merges"]:
        invs_b = _each(_bf, invs)
        lower = _each(lambda inv_b, a: _bf(_dot(inv_b, _bf(jnp.where(merge, a, 0.0)))), invs_b, a_list)
        invs = _each(lambda inv, low, inv_b: inv - _dot(low, inv_b), invs, lower, invs_b)
    return invs


def _gdn_tiles(conv, gt):
    hd = GDN_HEAD_DIM
    heads = range(GDN_HEADS)
    act = lambda base, h: _silu(conv(slice(base + h * hd, base + (h + 1) * hd)))
    l2 = lambda x: x * lax.rsqrt(jnp.sum(x * x, axis=-1, keepdims=True) + EPS)
    q = [l2(act(0, h)) * (hd ** -0.5) for h in heads]
    k = [l2(act(GDN_WIDTH, h)) for h in heads]
    v = [act(2 * GDN_WIDTH, h) for h in heads]
    beta = [gt["beta"][:, GB_COL + h:GB_COL + h + 1] for h in heads]
    gc = [gt["gc_col"][:, GA_COL + h:GA_COL + h + 1] for h in heads]
    gl = [gt["gl_col"][:, GA_COL + h:GA_COL + h + 1] for h in heads]
    gc_row = [gt["gc_row"][GA_COL + h:GA_COL + h + 1, :] for h in heads]

    decay = _each(lambda c, r: jnp.exp(jnp.where(gt["incl"], c - r, -jnp.inf)), gc, gc_row)
    kb = _each(_bf, k)
    kk = _each(_dot_nt, kb, kb)
    qk = _each(lambda q_, kb_, d: _dot_nt(_bf(q_), kb_) * d, q, kb, decay)
    a = _each(lambda kk_, d, b: jnp.where(gt["strict"], kk_ * d * b, 0.0), kk, decay, beta)
    egc = _each(jnp.exp, gc)
    rhs = _each(lambda v_, k_, b, e: jnp.concatenate([v_ * b, k_ * (b * e)], axis=1), v, k, beta, egc)
    inv_b = _each(_bf, _unit_lower_inverses(a, gt))
    x = _each(lambda i, r: _dot(i, _bf(r)), inv_b, rhs)
    resid = _each(lambda r, x_, a_: _bf(r - (x_ + _mm_near_f32(a_, x_))), rhs, x, a)
    x = _each(lambda x_, i, r: x_ + _dot(i, r), x, inv_b, resid)
    q_dec = _each(lambda q_, e: q_ * e, q, egc)
    k_dec = _each(lambda k_, l_, c: k_ * jnp.exp(l_ - c), k, gl, gc)
    return [x_[:, :hd] for x_ in x], [x_[:, hd:] for x_ in x], qk, q_dec, k_dec, gl


def _gdn_prompt_kernel(qkv_ref, sm_ref, smt_ref, gg_ref, cw_ref, alog_row, dtb_row, alog_col, dtb_col, nw_ref,
                       go_ref, s_out_ref, xbuf_ref, s_ref, *, cs):
    t = pl.program_id(1)
    t_rows = qkv_ref.shape[0]

    @pl.when(t == 0)
    def _():
        xbuf_ref[0:SUBLANES, :] = jnp.zeros((SUBLANES, GDN_CONV_DIM), F32)
        s_ref[...] = jnp.zeros_like(s_ref)

    xbuf_ref[SUBLANES:, :] = qkv_ref[...]

    def conv(cols):
        acc = xbuf_ref[SUBLANES:SUBLANES + t_rows, cols] * cw_ref[CONV_WIDTH - 1:CONV_WIDTH, cols]
        for w in range(CONV_WIDTH - 1):
            lo = SUBLANES - (CONV_WIDTH - 1) + w
            acc = acc + xbuf_ref[lo:lo + t_rows, cols] * cw_ref[w:w + 1, cols]
        return acc

    gt = _gdn_gates(sm_ref[...], smt_ref[...], alog_row[...], dtb_row[...], alog_col[...], dtb_col[...], cs)
    hd = GDN_HEAD_DIM
    heads = range(GDN_HEADS)
    u_base, k_cum, qk, q_dec, k_dec, gl = _gdn_tiles(conv, gt)
    states = [s_ref[h] for h in heads]
    us = [[] for _ in heads]
    ois = [[] for _ in heads]
    for ci in range(t_rows // cs):
        rows = slice(ci * cs, (ci + 1) * cs)
        sb = _each(_bf, states)
        u = [u_base[h][rows] - _dot(_bf(k_cum[h][rows]), sb[h]) for h in heads]
        for h in heads:
            ois[h].append(_dot(_bf(q_dec[h][rows]), sb[h]))
            us[h].append(u[h])
        states = [states[h] * jnp.exp(gl[h][ci * cs:ci * cs + 1, :]) + _dot_tn(_bf(k_dec[h][rows]), _bf(u[h]))
                  for h in heads]
    for h in heads:
        s_ref[h] = states[h]
        o = jnp.concatenate(ois[h], axis=0) + _dot(_bf(qk[h]), _bf(jnp.concatenate(us[h], axis=0)))
        cols = slice(h * hd, (h + 1) * hd)
        go_ref[:, cols] = _rms(o, nw_ref[...]) * _silu(gg_ref[:, cols])

    xbuf_ref[0:SUBLANES, :] = qkv_ref[t_rows - SUBLANES:, :]

    @pl.when(t == pl.num_programs(1) - 1)
    def _():
        s_out_ref[0] = s_ref[...]


def _gdn_prompt(qkv, sm, smt, gg, cw, alog_row, dtb_row, alog_col, dtb_col, nw, batch, tile):
    m = qkv.shape[0]
    steps = m // batch // tile
    row = lambda n: pl.BlockSpec((tile, n), lambda b, t: (b * steps + t, 0))
    full = lambda a: pl.BlockSpec(a.shape, lambda b, t: (0,) * a.ndim)
    hd = GDN_HEAD_DIM
    return pl.pallas_call(
        functools.partial(_gdn_prompt_kernel, cs=math.gcd(tile, GDN_CHUNK)), grid=(batch, steps),
        in_specs=[row(GDN_CONV_DIM), row(LANES), pl.BlockSpec((SMALL_COLS, tile), lambda b, t: (0, b * steps + t)),
                  row(GDN_WIDTH), full(cw), full(alog_row), full(dtb_row), full(alog_col), full(dtb_col), full(nw)],
        out_specs=(row(GDN_WIDTH), pl.BlockSpec((1, GDN_HEADS, hd, hd), lambda b, t: (b, 0, 0, 0))),
        out_shape=(jax.ShapeDtypeStruct((m, GDN_WIDTH), F32),
                   jax.ShapeDtypeStruct((batch, GDN_HEADS, hd, hd), F32)),
        scratch_shapes=[pltpu.VMEM((tile + SUBLANES, GDN_CONV_DIM), F32), pltpu.VMEM((GDN_HEADS, hd, hd), F32)],
        compiler_params=_params("arbitrary", "arbitrary"), name="gdn_prompt")(
            qkv, sm, smt, gg, cw, alog_row, dtb_row, alog_col, dtb_col, nw)


def _gdn_sample_kernel(qkv_ref, hist_ref, sm_ref, smt_ref, gg_ref, s0_ref, cw_ref, alog_row, dtb_row,
                       alog_col, dtb_col, nw_ref, *rest, ls):
    go_ref, s_out_ref, xbuf_ref = rest[-3:]
    bt = qkv_ref.shape[0]
    xbuf_ref[:, 0:SUBLANES, :] = hist_ref[...]
    xbuf_ref[:, SUBLANES:, :] = qkv_ref[...]

    def conv(cols):
        acc = xbuf_ref[:, SUBLANES:SUBLANES + ls, cols] * cw_ref[CONV_WIDTH - 1:CONV_WIDTH, cols]
        for w in range(CONV_WIDTH - 1):
            lo = SUBLANES - (CONV_WIDTH - 1) + w
            acc = acc + xbuf_ref[:, lo:lo + ls, cols] * cw_ref[w:w + 1, cols]
        return acc.reshape(bt * ls, acc.shape[-1])

    gt = _gdn_gates(sm_ref[...].reshape(bt * ls, LANES), smt_ref[...], alog_row[...], dtb_row[...],
                    alog_col[...], dtb_col[...], ls)
    u_base, k_cum, qk, q_dec, k_dec, gl = _gdn_tiles(conv, gt)
    for h in range(GDN_HEADS):
        us, ois = [], []
        for bi in range(bt):
            rows = slice(bi * ls, (bi + 1) * ls)
            state = s0_ref[0, bi, h]
            u = u_base[h][rows] - _dot(k_cum[h][rows], state)
            ois.append(_dot(q_dec[h][rows], state))
            s_out_ref[0, bi, h] = state * jnp.exp(gl[h][bi * ls:bi * ls + 1, :]) + _dot_tn(k_dec[h][rows], u)
            us.append(u)
        o = jnp.concatenate(ois, axis=0) + _dot(_bf(qk[h]), _bf(jnp.concatenate(us, axis=0)))
        hd = GDN_HEAD_DIM
        cols = slice(h * hd, (h + 1) * hd)
        gate = _silu(gg_ref[:, :, cols].reshape(bt * ls, hd))
        go_ref[:, :, cols] = (_rms(o, nw_ref[...]) * gate).reshape(bt, ls, hd)


def _gdn_sample(layer, qkv, hist, sm, smt, gg, s0_all, cw, alog_row, dtb_row, alog_col, dtb_col, nw, prev_states, bt):
    bs, ls, _ = qkv.shape
    per = lambda n: pl.BlockSpec((bt, ls, n), lambda i: (i, 0, 0))
    full = lambda a: pl.BlockSpec(a.shape, lambda i: (0,) * a.ndim)
    hd = GDN_HEAD_DIM
    st = pl.BlockSpec((1, bt, GDN_HEADS, hd, hd), lambda i: (layer, i, 0, 0, 0))
    args = [qkv, hist, sm, smt, gg, s0_all, cw, alog_row, dtb_row, alog_col, dtb_col, nw]
    in_specs = [per(GDN_CONV_DIM), pl.BlockSpec((bt, SUBLANES, GDN_CONV_DIM), lambda i: (i, 0, 0)), per(LANES),
                pl.BlockSpec((SMALL_COLS, bt * ls), lambda i: (0, i)), per(GDN_WIDTH), st,
                full(cw), full(alog_row), full(dtb_row), full(alog_col), full(dtb_col), full(nw)]
    aliases = {}
    if prev_states is not None:
        aliases = {len(args): 1}
        args.append(prev_states)
        in_specs.append(pl.BlockSpec(memory_space=pl.ANY))
    return pl.pallas_call(
        functools.partial(_gdn_sample_kernel, ls=ls), grid=(bs // bt,), in_specs=in_specs,
        out_specs=(per(GDN_WIDTH), st),
        out_shape=(jax.ShapeDtypeStruct((bs, ls, GDN_WIDTH), F32), jax.ShapeDtypeStruct(s0_all.shape, F32)),
        scratch_shapes=[pltpu.VMEM((bt, SUBLANES + ls, GDN_CONV_DIM), F32)],
        input_output_aliases=aliases, compiler_params=_params("arbitrary"), name="gdn_sample")(*args)


def _norm_matmul_kernel(x_ref, g_ref, w_ref, *o_refs):
    xn = _rms(x_ref[...], g_ref[...]).astype(BF16)
    n = w_ref.shape[1] // len(o_refs)
    for i, o_ref in enumerate(o_refs):
        o_ref[...] = _dot(xn, w_ref[:, i * n:(i + 1) * n])


def _norm_matmul(x, g, w, n_out, tm):
    m = x.shape[0]
    n = w.shape[1] // n_out
    return pl.pallas_call(
        _norm_matmul_kernel, grid=(m // tm,),
        in_specs=[pl.BlockSpec((tm, x.shape[1]), lambda i: (i, 0)), pl.BlockSpec(g.shape, lambda i: (0, 0)),
                  pl.BlockSpec(w.shape, lambda i: (0, 0))],
        out_specs=tuple(pl.BlockSpec((tm, n), lambda i: (i, 0)) for _ in range(n_out)),
        out_shape=tuple(jax.ShapeDtypeStruct((m, n), F32) for _ in range(n_out)),
        compiler_params=_params("arbitrary"), name="norm_matmul")(x, g, w)


def _matmul_residual_kernel(*refs, n_in):
    x_ref, o_ref = refs[0], refs[-1]
    acc = x_ref[...]
    for a_ref, w_ref in zip(refs[1:1 + n_in], refs[1 + n_in:1 + 2 * n_in]):
        acc = acc + _dot(a_ref[...].astype(BF16), w_ref[...])
    o_ref[...] = acc


def _matmul_residual(x, a_list, w_list, tm):
    m, d = x.shape
    n_in = len(a_list)
    in_specs = [pl.BlockSpec((tm, d), lambda i: (i, 0))]
    in_specs += [pl.BlockSpec((tm, a.shape[1]), lambda i: (i, 0)) for a in a_list]
    in_specs += [pl.BlockSpec(w.shape, lambda i: (0, 0)) for w in w_list]
    return pl.pallas_call(
        functools.partial(_matmul_residual_kernel, n_in=n_in), grid=(m // tm,), in_specs=in_specs,
        out_specs=pl.BlockSpec((tm, d), lambda i: (i, 0)), out_shape=jax.ShapeDtypeStruct((m, d), F32),
        compiler_params=_params("arbitrary"), name="matmul_residual")(x, *a_list, *w_list)


def _mem_attn_prompt_kernel(x_ref, g_ref, wq_ref, wo_ref, mk_ref, mv_ref, o_ref):
    x = x_ref[0]
    xn = _rms(x, g_ref[...]).astype(BF16)
    q = _dot(xn, wq_ref[...]).astype(BF16)
    outs = []
    for h in range(MEM_HEADS):
        cols = slice(h * MEM_HEAD_DIM, (h + 1) * MEM_HEAD_DIM)
        s = _dot_nt(q[:, cols], mk_ref[0, :, cols].astype(BF16)) * (MEM_HEAD_DIM ** -0.5)
        p = jnp.exp(s - jnp.max(s, axis=1, keepdims=True))
        o = _dot(p.astype(BF16), mv_ref[0, :, cols].astype(BF16))
        outs.append(o / jnp.sum(p, axis=1, keepdims=True))
    o_ref[0] = x + _dot(jnp.concatenate(outs, axis=1).astype(BF16), wo_ref[...])


def _mem_attn_prompt(x, g, wq, wo, mk, mv, tm):
    b, l, d = x.shape
    n_mem = mk.shape[1]
    full = lambda a: pl.BlockSpec(a.shape, lambda bi, i: (0,) * a.ndim)
    mem_spec = pl.BlockSpec((1, n_mem, MEM_WIDTH), lambda bi, i: (bi, 0, 0))
    x_spec = pl.BlockSpec((1, tm, d), lambda bi, i: (bi, i, 0))
    return pl.pallas_call(
        _mem_attn_prompt_kernel, grid=(b, l // tm),
        in_specs=[x_spec, full(g), full(wq), full(wo), mem_spec, mem_spec],
        out_specs=x_spec, out_shape=jax.ShapeDtypeStruct(x.shape, F32),
        compiler_params=_params("arbitrary", "arbitrary"), name="mem_attn_prompt")(x, g, wq, wo, mk, mv)


def _mem_attn_sample_kernel(q_ref, mk_ref, mv_ref, o_ref):
    bt, ls, _ = q_ref.shape
    rows = MEM_HEADS * ls
    n = mk_ref.shape[2]
    hd = MEM_HEAD_DIM
    own_head = _mod(_iota((rows, n), 1), MEM_HEADS) == _div(_iota((rows, n), 0), ls)
    for bi in range(bt):
        q = q_ref[bi]
        qs = jnp.concatenate([q[:, h * hd:(h + 1) * hd] for h in range(MEM_HEADS)], axis=0)
        s = _dot_nt(qs.astype(BF16), mk_ref[0, bi].astype(BF16)) * (hd ** -0.5)
        s = jnp.where(own_head, s, -jnp.inf)
        p = jnp.exp(s - jnp.max(s, axis=1, keepdims=True))
        o = _dot(p.astype(BF16), mv_ref[0, bi].astype(BF16)) / jnp.sum(p, axis=1, keepdims=True)
        o_ref[bi] = jnp.concatenate([o[h * ls:(h + 1) * ls] for h in range(MEM_HEADS)], axis=1)


def _mem_attn_sample(layer, q, mk_all, mv_all, bt):
    bs, ls, _ = q.shape
    mem_spec = pl.BlockSpec((1, bt) + mk_all.shape[2:], lambda i: (layer, i, 0, 0))
    q_spec = pl.BlockSpec((bt, ls, MEM_WIDTH), lambda i: (i, 0, 0))
    return pl.pallas_call(
        _mem_attn_sample_kernel, grid=(bs // bt,), in_specs=[q_spec, mem_spec, mem_spec],
        out_specs=q_spec, out_shape=jax.ShapeDtypeStruct(q.shape, F32),
        compiler_params=_params("arbitrary"), name="mem_attn_sample")(q, mk_all, mv_all)


def _ffn_kernel(x_ref, g_ref, wa_ref, wu_ref, wo_ref, gf_ref, o_ref, *, chunk, final_norm):
    x = x_ref[...]
    xn = _rms(x, g_ref[...]).astype(BF16)
    acc = x
    for c in range(wa_ref.shape[1] // chunk):
        cols = slice(c * chunk, (c + 1) * chunk)
        a = _dot(xn, wa_ref[:, cols])
        u = _dot(xn, wu_ref[:, cols])
        acc = acc + _dot((_silu(a) * u).astype(BF16), wo_ref[cols, :])
    if final_norm:
        acc = _rms(acc, gf_ref[...])
    o_ref[...] = acc


def _ffn(x, g, wa, wu, wo, gf, final_norm, tm):
    m, d = x.shape
    full = lambda a: pl.BlockSpec(a.shape, lambda i: (0,) * a.ndim)
    row = pl.BlockSpec((tm, d), lambda i: (i, 0))
    return pl.pallas_call(
        functools.partial(_ffn_kernel, chunk=2 * LANES, final_norm=final_norm), grid=(m // tm,),
        in_specs=[row, full(g), full(wa), full(wu), full(wo), full(gf)], out_specs=row,
        out_shape=jax.ShapeDtypeStruct((m, d), F32), compiler_params=_params("arbitrary"), name="ffn")(
            x, g, wa, wu, wo, gf)


def _pick(m, cap):
    t = min(m, cap)
    while m % t:
        t -= SUBLANES
    assert t > 0 and m % t == 0, (m, cap)
    return t


def _pad_lanes(v, offset):
    row = jnp.zeros((1, LANES), F32).at[0, offset:offset + v.shape[0]].set(v)
    return row, row[0, :SMALL_COLS].reshape(SMALL_COLS, 1)


def kernel(x_prompt, x_sample, cache_fox_k, cache_fox_v, cache_fox_logf, state_gdn, state_gdn_conv, cache_mem_k, cache_mem_v, page_table, mem_prompt, g_norm_mix, w_in, b_fox_f, gdn_conv_w, gdn_a_log, gdn_dt_bias, gdn_norm_w, w_out, g_norm_memin, w_mem_kv, g_norm_mem, w_mem_q, w_mem_o, g_norm_ffn, w_ffn_in, w_ffn_out, g_final):
    bp, lp, d = x_prompt.shape
    bs, ls, _ = x_sample.shape
    depth = w_in.shape[0]
    n_phys, page = cache_fox_k.shape[1], cache_fox_k.shape[2]
    n_pages = page_table.shape[1]
    n_mem = mem_prompt.shape[1]
    d_ff = w_ffn_out.shape[1]
    mp, ms = bp * lp, bs * ls
    hd = GDN_HEAD_DIM

    yp = x_prompt.reshape(mp, d)
    ys = x_sample.reshape(ms, d)
    mem_flat = mem_prompt.reshape(bp * n_mem, d)
    cache_k = cache_fox_k.transpose(0, 1, 3, 4, 2).reshape(depth, n_phys, FOX_WIDTH, page)
    cache_v = cache_fox_v.transpose(0, 1, 3, 4, 2).reshape(depth, n_phys, FOX_WIDTH, page)
    page_rows = depth * n_phys * FOX_HEADS
    cache_cum = _cumsum_pages(cache_fox_logf.transpose(0, 1, 3, 2).reshape(page_rows, page), _pick(page_rows, 512))
    cache_cum = cache_cum.reshape(depth, n_phys, FOX_HEADS, page)
    tokens_per_row = LANES // FOX_HEADS
    hist = jnp.pad(state_gdn_conv, ((0, 0), (0, 0), (SUBLANES - (CONV_WIDTH - 1), 0), (0, 0)))
    mem_k_cache = cache_mem_k.reshape(depth, bs, n_mem * MEM_HEADS, MEM_HEAD_DIM)
    mem_v_cache = cache_mem_v.reshape(depth, bs, n_mem * MEM_HEADS, MEM_HEAD_DIM)

    o1 = FOX_WIDTH * 3
    o2 = o1 + FOX_HEADS
    o3 = o2 + GDN_CONV_DIM
    o4 = o3 + 2 * GDN_HEADS
    row2 = lambda v: v.reshape(1, -1)

    outs_p = {k: [] for k in ("lf", "gs", "gc", "mk", "mv")}
    outs_s = {k: [] for k in ("fk", "fv", "lf", "gc")}
    kt_all = vt_all = states_s = None
    for l in range(depth):
        w = w_in[l]
        wq = w[:, :FOX_WIDTH].astype(BF16)
        wkv = w[:, FOX_WIDTH:o1].astype(BF16)
        wkvt = w[:, FOX_WIDTH:o1].T.astype(BF16)
        wg = jnp.concatenate([w[:, o2:o3], w[:, o4:]], axis=1).astype(BF16)
        w_small = jnp.concatenate([w[:, o1:o2], w[:, o3:o4]], axis=1)
        ws = jnp.pad(w_small, ((0, 0), (0, LANES - SMALL_COLS))).astype(BF16)
        wst = w_small.T.astype(BF16)
        b_row = row2(b_fox_f[l])
        b_col = b_fox_f[l].reshape(FOX_HEADS, 1)
        b_pad = jnp.pad(b_row, ((0, 0), (0, LANES - FOX_HEADS)))
        alog_row, alog_col = _pad_lanes(gdn_a_log[l], GA_COL)
        dtb_row, dtb_col = _pad_lanes(gdn_dt_bias[l], GA_COL)
        cw = gdn_conv_w[l]
        nw = row2(gdn_norm_w[l])
        wo_fox = w_out[l][:FOX_WIDTH].astype(BF16)
        wo_gdn = w_out[l][FOX_WIDTH:].astype(BF16)
        w_kv = w_mem_kv[l].astype(BF16)
        w_q = w_mem_q[l].astype(BF16)
        w_o = w_mem_o[l].astype(BF16)
        wa = w_ffn_in[l][:, :d_ff].astype(BF16)
        wu = w_ffn_in[l][:, d_ff:].astype(BF16)
        wf = w_ffn_out[l].astype(BF16)
        g_mix, g_memin, g_mem, g_ffn = row2(g_norm_mix[l]), row2(g_norm_memin[l]), row2(g_norm_mem[l]), row2(g_norm_ffn[l])
        gf = row2(g_final)
        last = l == depth - 1

        fqb, kt_all, vt_all, ktb, vtb, qkv, gg, lf, lft, sm, smt = _in_proj_prompt(
            l, depth, bp, yp, g_mix, wq, wkvt, wg, ws, wst, b_row, b_col, kt_all, vt_all, _pick(lp, 256))
        seg = lp // tokens_per_row
        c_col = _cumsum_tokens(lf.reshape(mp // tokens_per_row, LANES), seg, seg).reshape(bp, lp, FOX_HEADS)
        c_row = _cumsum_lanes(lft, _pick(lp, 512))
        fo = _fox_prompt(fqb.reshape(bp, lp, FOX_WIDTH), ktb, vtb, c_col, c_row, _pick(lp, 512)).reshape(mp, FOX_WIDTH)
        go, s_new = _gdn_prompt(qkv, sm, smt, gg, cw, alog_row, dtb_row, alog_col, dtb_col, nw, bp, _pick(lp, 256))
        yp = _matmul_residual(yp, [fo, go], [wo_fox, wo_gdn], _pick(mp, 512))
        mk, mv = _norm_matmul(mem_flat, g_memin, w_kv, 2, _pick(bp * n_mem, 256))
        yp = _mem_attn_prompt(yp.reshape(bp, lp, d), g_mem, w_q, w_o, mk.reshape(bp, n_mem, MEM_WIDTH),
                              mv.reshape(bp, n_mem, MEM_WIDTH), _pick(lp, 512)).reshape(mp, d)
        yp = _ffn(yp, g_ffn, wa, wu, wf, gf, last, _pick(mp, 256))
        outs_p["lf"].append(lft.transpose(0, 2, 1))
        outs_p["gs"].append(s_new)
        outs_p["gc"].append(qkv.reshape(bp, lp, GDN_CONV_DIM)[:, lp - (CONV_WIDTH - 1):])
        outs_p["mk"].append(mk.reshape(bp, n_mem, MEM_HEADS, MEM_HEAD_DIM))
        outs_p["mv"].append(mv.reshape(bp, n_mem, MEM_HEADS, MEM_HEAD_DIM))

        fqb, fk, fv, qkv, gg, lf, sm, smt = _in_proj_sample(ys, g_mix, wq, wkv, wg, ws, wst, b_row, _pick(ms, 256))
        s3 = lambda a: a.reshape(bs, ls, a.shape[-1])
        fo = _fox_decode(l, page_table, s3(fqb), s3(fk), s3(fv), s3(sm), b_pad, cache_k, cache_v,
                         cache_cum).reshape(ms, FOX_WIDTH)
        go, states_s = _gdn_sample(l, s3(qkv), hist[l], s3(sm), smt, s3(gg), state_gdn, cw, alog_row, dtb_row,
                                   alog_col, dtb_col, nw, states_s, _pick(bs, 16))
        ys = _matmul_residual(ys, [fo, go.reshape(ms, GDN_WIDTH)], [wo_fox, wo_gdn], _pick(ms, 512))
        (q_mem,) = _norm_matmul(ys, g_mem, w_q, 1, _pick(ms, 256))
        o_mem = _mem_attn_sample(l, s3(q_mem), mem_k_cache, mem_v_cache, _pick(bs, 8)).reshape(ms, MEM_WIDTH)
        ys = _matmul_residual(ys, [o_mem], [w_o], _pick(ms, 512))
        ys = _ffn(ys, g_ffn, wa, wu, wf, gf, last, _pick(ms, 256))
        outs_s["fk"].append(fk.reshape(bs, ls, FOX_HEADS, FOX_HEAD_DIM))
        outs_s["fv"].append(fv.reshape(bs, ls, FOX_HEADS, FOX_HEAD_DIM))
        outs_s["lf"].append(lf.reshape(bs, ls, FOX_HEADS))
        outs_s["gc"].append(s3(qkv)[:, ls - (CONV_WIDTH - 1):])

    st = jnp.stack
    heads_last = lambda a: a.reshape(depth, bp, FOX_HEADS, FOX_HEAD_DIM, lp).transpose(0, 1, 4, 2, 3)
    return (yp.reshape(bp, lp, d), ys.reshape(bs, ls, d),
            heads_last(kt_all), heads_last(vt_all), st(outs_p["lf"]), st(outs_p["gs"]), st(outs_p["gc"]),
            st(outs_p["mk"]), st(outs_p["mv"]),
            st(outs_s["fk"]), st(outs_s["fv"]), st(outs_s["lf"]), states_s, st(outs_s["gc"]))
```

```python
import functools
import math

import jax
import jax.numpy as jnp
from jax import lax
from jax.experimental import pallas as pl
from jax.experimental.pallas import tpu as pltpu

F32 = jnp.float32
BF16 = jnp.bfloat16
HI = lax.Precision.HIGHEST

EPS = 1e-6
D_MODEL = 1024
FOX_HEADS = 8
FOX_HEAD_DIM = 64
FOX_WIDTH = FOX_HEADS * FOX_HEAD_DIM
GDN_HEADS = 4
GDN_HEAD_DIM = 128
GDN_WIDTH = GDN_HEADS * GDN_HEAD_DIM
GDN_CONV_DIM = 3 * GDN_WIDTH
CONV_WIDTH = 4
GDN_CHUNK = 64
MEM_HEADS = 4
MEM_HEAD_DIM = 128
MEM_WIDTH = MEM_HEADS * MEM_HEAD_DIM
LANES = 128
SUBLANES = 8
SMALL_COLS = 16
GA_COL = FOX_HEADS
GB_COL = FOX_HEADS + GDN_HEADS
VMEM_LIMIT = 56 * 1024 * 1024


def _params(*sem):
    return pltpu.CompilerParams(dimension_semantics=sem, vmem_limit_bytes=VMEM_LIMIT)


def _dot(a, b):
    return jnp.dot(a, b, preferred_element_type=F32)


def _dot_nt(a, b):
    return lax.dot_general(a, b, (((1,), (1,)), ((), ())), preferred_element_type=F32)


def _dot_tn(a, b):
    return lax.dot_general(a, b, (((0,), (0,)), ((), ())), preferred_element_type=F32)


def _dot_hi(a, b):
    return jnp.dot(a, b, precision=HI, preferred_element_type=F32)


def _dot_nt_hi(a, b):
    return lax.dot_general(a, b, (((1,), (1,)), ((), ())), precision=HI, preferred_element_type=F32)


def _rms(x, g):
    return x * lax.rsqrt(jnp.mean(x * x, axis=-1, keepdims=True) + EPS) * g


def _log_sigmoid(x):
    return jnp.minimum(x, 0.0) - jnp.log1p(jnp.exp(-jnp.abs(x)))


def _softplus(x):
    return jnp.maximum(x, 0.0) + jnp.log1p(jnp.exp(-jnp.abs(x)))


def _silu(x):
    return x * jax.nn.sigmoid(x)


def _iota(shape, dim):
    return lax.broadcasted_iota(jnp.int32, shape, dim)


def _log2(n):
    lg = n.bit_length() - 1
    assert n == 1 << lg, n
    return lg


def _div(x, n):
    return x >> _log2(n)


def _mod(x, n):
    _log2(n)
    return x & (n - 1)


def _in_proj_common(x_ref, g_ref, wq_ref, wg_ref, ws_ref, wst_ref, b_ref,
                    fqb_ref, qkv_ref, gg_ref, lf_ref, sm_ref, smt_ref):
    xn = _rms(x_ref[...], g_ref[...]).astype(BF16)
    fqb_ref[...] = (_dot(xn, wq_ref[...]) * (FOX_HEAD_DIM ** -0.5)).astype(BF16)
    qkv_ref[...] = _dot(xn, wg_ref[:, 0:GDN_CONV_DIM])
    gg_ref[...] = _dot(xn, wg_ref[:, GDN_CONV_DIM:])
    sm = _dot(xn, ws_ref[...])
    smt = _dot_nt(wst_ref[...], xn)
    sm_ref[...] = sm
    smt_ref[...] = smt
    lf_ref[...] = _log_sigmoid(sm[:, 0:FOX_HEADS] + b_ref[...])
    return xn, smt


LOG2E = 1.4426950408889634
AUG_ROWS = SUBLANES


def _in_proj_prompt_kernel(x_ref, g_ref, wqt_ref, wk_ref, wkvt_ref, wg_ref, ws_ref, wst_ref, b_ref, bcol_ref,
                           *rest, tiles_per_seq):
    (kt_ref, vt_ref, kaug_ref, qtaug_ref, vtaug_ref, qkv_ref, gg_ref, lft_ref, sm_ref, smt_ref,
     crow_ref, ccol_ref) = rest[-12:]
    tm = x_ref.shape[0]
    hd = FOX_HEAD_DIM

    @pl.when(pl.program_id(0) % tiles_per_seq == 0)
    def _():
        crow_ref[...] = jnp.zeros_like(crow_ref)
        ccol_ref[...] = jnp.zeros_like(ccol_ref)

    xn = _rms(x_ref[...], g_ref[...]).astype(BF16)
    qkv_ref[...] = _dot(xn, wg_ref[:, 0:GDN_CONV_DIM])
    gg_ref[...] = _dot(xn, wg_ref[:, GDN_CONV_DIM:])
    sm = _dot(xn, ws_ref[...])
    smt = _dot_nt(wst_ref[...], xn)
    sm_ref[...] = sm
    smt_ref[...] = smt
    lf_col = _log_sigmoid(sm[:, 0:FOX_HEADS] + b_ref[...])
    lf_row = _log_sigmoid(smt[0:FOX_HEADS, :] + bcol_ref[...])
    lft_ref[0] = lf_row
    kt = _dot_nt(wkvt_ref[0:FOX_WIDTH, :], xn)
    kt_ref[0, 0] = kt
    vt = _dot_nt(wkvt_ref[FOX_WIDTH:, :], xn)
    vt_ref[0, 0] = vt
    k_tok = _dot(xn, wk_ref[...])
    qt = _dot_nt(wqt_ref[...], xn) * (hd ** -0.5 * LOG2E)

    r = _iota((tm, tm), 0)
    c = _iota((tm, tm), 1)
    c_col = sum(_dot(_bf(c <= r), part) for part in _split3(lf_col)) + crow_ref[...]
    c_row = sum(_dot(part, _bf(r <= c)) for part in _split3(lf_row)) + ccol_ref[...]
    crow_ref[...] = c_col[tm - 1:tm, :]
    ccol_ref[...] = c_row[:, tm - 1:tm]

    lane = _iota((tm, LANES), 1)
    sub = _iota((AUG_ROWS, tm), 0)
    ones_key = jnp.where((lane >= hd + 3) & (lane < hd + 6), 1.0, 0.0)
    zeros_tail = jnp.zeros((LANES - hd - AUG_ROWS, tm), F32)
    value_tail = jnp.where(sub == 0, 1.0, 0.0)
    for h in range(FOX_HEADS):
        pair = k_tok[:, (h // 2) * LANES:(h // 2 + 1) * LANES]
        k_h = pair if h % 2 == 0 else pltpu.roll(pair, hd, 1)
        k1, k2, k3 = [p.astype(F32) for p in _split3(c_col[:, h:h + 1] * (-LOG2E))]
        extra = jnp.where(lane == hd, k1, jnp.where(lane == hd + 1, k2, jnp.where(lane == hd + 2, k3, ones_key)))
        kaug_ref[0, h] = jnp.where(lane < hd, k_h, extra).astype(BF16)
        q1, q2, q3 = [p.astype(F32) for p in _split3(c_row[h:h + 1, :] * LOG2E)]
        query_tail = jnp.where(sub < 3, 1.0, jnp.where(sub == 3, q1, jnp.where(sub == 4, q2,
                                                                           jnp.where(sub == 5, q3, 0.0))))
        rows = slice(h * hd, (h + 1) * hd)
        qtaug_ref[0, h] = jnp.concatenate([qt[rows], query_tail, zeros_tail], axis=0).astype(BF16)
        vtaug_ref[0, h] = jnp.concatenate([vt[rows], value_tail, zeros_tail], axis=0).astype(BF16)


def _in_proj_prompt(layer, depth, batch, x, g, wqt, wk, wkvt, wg, ws, wst, b_row, b_col, prev_kt, prev_vt, tm):
    m = x.shape[0]
    l = m // batch
    nt = l // tm
    row = lambda n: pl.BlockSpec((tm, n), lambda i: (i, 0))
    full = lambda a: pl.BlockSpec(a.shape, lambda i: (0,) * a.ndim)
    kv_all = pl.BlockSpec((1, 1, FOX_WIDTH, tm), lambda i: (layer, i // nt, 0, i % nt))
    feat_major = pl.BlockSpec((1, FOX_HEADS, LANES, tm), lambda i: (i // nt, 0, 0, i % nt))
    out_shape = (
        jax.ShapeDtypeStruct((depth, batch, FOX_WIDTH, l), F32),
        jax.ShapeDtypeStruct((depth, batch, FOX_WIDTH, l), F32),
        jax.ShapeDtypeStruct((batch, FOX_HEADS, l, LANES), BF16),
        jax.ShapeDtypeStruct((batch, FOX_HEADS, LANES, l), BF16),
        jax.ShapeDtypeStruct((batch, FOX_HEADS, LANES, l), BF16),
        jax.ShapeDtypeStruct((m, GDN_CONV_DIM), F32),
        jax.ShapeDtypeStruct((m, GDN_WIDTH), F32),
        jax.ShapeDtypeStruct((batch, FOX_HEADS, l), F32),
        jax.ShapeDtypeStruct((m, LANES), F32),
        jax.ShapeDtypeStruct((SMALL_COLS, m), F32),
    )
    out_specs = (kv_all, kv_all,
                 pl.BlockSpec((1, FOX_HEADS, tm, LANES), lambda i: (i // nt, 0, i % nt, 0)), feat_major, feat_major,
                 row(GDN_CONV_DIM), row(GDN_WIDTH),
                 pl.BlockSpec((1, FOX_HEADS, tm), lambda i: (i // nt, 0, i % nt)), row(LANES),
                 pl.BlockSpec((SMALL_COLS, tm), lambda i: (0, i)))
    args = [x, g, wqt, wk, wkvt, wg, ws, wst, b_row, b_col]
    in_specs = [row(D_MODEL)] + [full(a) for a in args[1:]]
    aliases = {}
    if prev_kt is not None:
        aliases = {len(args): 0, len(args) + 1: 1}
        args += [prev_kt, prev_vt]
        in_specs += [pl.BlockSpec(memory_space=pl.ANY)] * 2
    return pl.pallas_call(
        functools.partial(_in_proj_prompt_kernel, tiles_per_seq=nt), grid=(m // tm,), in_specs=in_specs,
        out_specs=out_specs, out_shape=out_shape, input_output_aliases=aliases,
        scratch_shapes=[pltpu.VMEM((1, FOX_HEADS), F32), pltpu.VMEM((FOX_HEADS, 1), F32)],
        compiler_params=_params("arbitrary"), name="in_proj_prompt")(*args)


def _in_proj_sample_kernel(x_ref, g_ref, wq_ref, wkv_ref, wg_ref, ws_ref, wst_ref, b_ref,
                           fqb_ref, fk_ref, fv_ref, qkv_ref, gg_ref, lf_ref, sm_ref, smt_ref):
    xn, _ = _in_proj_common(x_ref, g_ref, wq_ref, wg_ref, ws_ref, wst_ref, b_ref,
                            fqb_ref, qkv_ref, gg_ref, lf_ref, sm_ref, smt_ref)
    fk_ref[...] = _dot(xn, wkv_ref[:, 0:FOX_WIDTH])
    fv_ref[...] = _dot(xn, wkv_ref[:, FOX_WIDTH:])


def _in_proj_sample(x, g, wq, wkv, wg, ws, wst, b_row, tm):
    m = x.shape[0]
    row = lambda n: pl.BlockSpec((tm, n), lambda i: (i, 0))
    full = lambda a: pl.BlockSpec(a.shape, lambda i: (0,) * a.ndim)
    out_shape = (
        jax.ShapeDtypeStruct((m, FOX_WIDTH), BF16),
        jax.ShapeDtypeStruct((m, FOX_WIDTH), F32),
        jax.ShapeDtypeStruct((m, FOX_WIDTH), F32),
        jax.ShapeDtypeStruct((m, GDN_CONV_DIM), F32),
        jax.ShapeDtypeStruct((m, GDN_WIDTH), F32),
        jax.ShapeDtypeStruct((m, FOX_HEADS), F32),
        jax.ShapeDtypeStruct((m, LANES), F32),
        jax.ShapeDtypeStruct((SMALL_COLS, m), F32),
    )
    out_specs = (row(FOX_WIDTH), row(FOX_WIDTH), row(FOX_WIDTH), row(GDN_CONV_DIM), row(GDN_WIDTH),
                 row(FOX_HEADS), row(LANES), pl.BlockSpec((SMALL_COLS, tm), lambda i: (0, i)))
    args = [x, g, wq, wkv, wg, ws, wst, b_row]
    return pl.pallas_call(
        _in_proj_sample_kernel, grid=(m // tm,), in_specs=[row(D_MODEL)] + [full(a) for a in args[1:]],
        out_specs=out_specs, out_shape=out_shape, compiler_params=_params("arbitrary"),
        name="in_proj_sample")(*args)


def _cumsum_pages_kernel(x_ref, o_ref):
    n = x_ref.shape[1]
    upto = (_iota((n, n), 0) <= _iota((n, n), 1)).astype(F32)
    o_ref[...] = _dot_hi(x_ref[...], upto)


def _cumsum_pages(x, block_rows):
    rows, n = x.shape
    spec = pl.BlockSpec((block_rows, n), lambda i: (i, 0))
    return pl.pallas_call(
        _cumsum_pages_kernel, grid=(rows // block_rows,), in_specs=[spec], out_specs=spec,
        out_shape=jax.ShapeDtypeStruct(x.shape, F32), compiler_params=_params("arbitrary"),
        name="cumsum_pages")(x)


FOX_UNIT_Q = 256
FOX_UNIT_GROUP = 4


def _fox_prompt_kernel(k_ref, qt_ref, vt_ref, o_ref, m_ref, acc_ref, *, tq, tk):
    qi = pl.program_id(1)
    ki = pl.program_id(2)
    uq = min(FOX_UNIT_Q, tq)
    hd = FOX_HEAD_DIM

    @pl.when(ki == 0)
    def _():
        m_ref[...] = jnp.full_like(m_ref, -jnp.inf)
        acc_ref[...] = jnp.zeros_like(acc_ref)

    units = [(h, qc) for h in range(FOX_HEADS) for qc in range(tq // uq)]

    def step(diagonal):
        if diagonal:
            key_idx = _iota((tk, uq), 0)
            qry_idx = _iota((tk, uq), 1)
        for g0 in range(0, len(units), FOX_UNIT_GROUP):
            group = units[g0:g0 + FOX_UNIT_GROUP]
            cols = [slice(qc * uq, (qc + 1) * uq) for _, qc in group]
            ts = [_dot(k_ref[0, h], qt_ref[0, h, :, cs_]) for (h, _), cs_ in zip(group, cols)]
            if diagonal:
                ts = [jnp.where(key_idx <= qry_idx + qc * uq, t, -jnp.inf) for t, (_, qc) in zip(ts, group)]
            m_old = [m_ref[h, :, cs_] for (h, _), cs_ in zip(group, cols)]
            m_new = [jnp.maximum(mo, jnp.max(t, axis=0, keepdims=True)) for mo, t in zip(m_old, ts)]
            ps = [jnp.exp2(t - mn).astype(BF16) for t, mn in zip(ts, m_new)]
            pvs = [_dot(vt_ref[0, h], p) for (h, _), p in zip(group, ps)]
            for (h, _), cs_, mo, mn, pv in zip(group, cols, m_old, m_new, pvs):
                acc_ref[h, :, cs_] = jnp.exp2(mo - mn) * acc_ref[h, :, cs_] + pv
                m_ref[h, :, cs_] = mn

    @pl.when(ki < qi)
    def _():
        step(False)

    @pl.when(ki == qi)
    def _():
        step(True)
        for j in range(FOX_HEADS // 2):
            pair = [acc_ref[h, 0:hd, :] / acc_ref[h, hd:hd + 1, :] for h in (2 * j, 2 * j + 1)]
            o_ref[0, :, j * LANES:(j + 1) * LANES] = jnp.concatenate(pair, axis=0).T


def _fox_prompt(kaug, qtaug, vtaug, tq):
    b, _, l, _ = kaug.shape
    tk = tq
    n = l // tq
    return pl.pallas_call(
        functools.partial(_fox_prompt_kernel, tq=tq, tk=tk), grid=(b, n, n),
        in_specs=[pl.BlockSpec((1, FOX_HEADS, tk, LANES), lambda bi, qi, ki: (bi, 0, jnp.minimum(ki, qi), 0)),
                  pl.BlockSpec((1, FOX_HEADS, LANES, tq), lambda bi, qi, ki: (bi, 0, 0, qi)),
                  pl.BlockSpec((1, FOX_HEADS, LANES, tk), lambda bi, qi, ki: (bi, 0, 0, jnp.minimum(ki, qi)))],
        out_specs=pl.BlockSpec((1, tq, FOX_WIDTH), lambda bi, qi, ki: (bi, qi, 0)),
        out_shape=jax.ShapeDtypeStruct((b, l, FOX_WIDTH), F32),
        scratch_shapes=[pltpu.VMEM((FOX_HEADS, 1, tq), F32), pltpu.VMEM((FOX_HEADS, LANES, tq), F32)],
        compiler_params=_params("arbitrary", "arbitrary", "arbitrary"), name="fox_prompt")(kaug, qtaug, vtaug)


def _fox_decode_kernel(pt_ref, q_ref, kn_ref, vn_ref, sm_ref, b_ref, *rest, n_pages, page, ls):
    del pt_ref
    k_refs, v_refs, c_refs = rest[0:n_pages], rest[n_pages:2 * n_pages], rest[2 * n_pages:3 * n_pages]
    o_ref = rest[3 * n_pages]
    rows = FOX_HEADS * ls
    headmask = _div(_iota((rows, FOX_WIDTH), 0), ls) == _div(_iota((rows, FOX_WIDTH), 1), FOX_HEAD_DIM)
    expand = (_div(_iota((rows, FOX_HEADS), 0), ls) == _iota((rows, FOX_HEADS), 1)).astype(F32)
    q = q_ref[0].astype(F32)
    qbd = jnp.where(headmask, jnp.concatenate([q] * FOX_HEADS, axis=0), 0.0).astype(BF16)

    lane = _iota((ls, LANES), 1)
    lf_new = jnp.where(lane < FOX_HEADS, _log_sigmoid(sm_ref[0] + b_ref[...]), 0.0)
    tok = _iota((ls, LANES), 0)
    cum_new = lf_new
    for s in range(1, ls):
        cum_new = cum_new + jnp.where(tok >= s, pltpu.roll(lf_new, s, 0), 0.0)
    cum_new = cum_new[:, 0:FOX_HEADS]
    query_of_row = (_mod(_iota((rows, ls), 0), ls) == _iota((rows, ls), 1)).astype(F32)
    cq_row = jnp.sum(_dot_hi(query_of_row, cum_new) * expand, axis=1, keepdims=True)

    vis = _iota((rows, ls), 1) <= _mod(_iota((rows, ls), 0), ls)
    tiles = [jnp.where(vis, _dot_nt(qbd, kn_ref[0].astype(BF16)) - _dot_nt_hi(expand, cum_new), -jnp.inf)]
    reach = jnp.zeros((FOX_HEADS, 1), F32)
    for r in range(n_pages):
        cum_page = c_refs[r][0, 0]
        reach = reach + cum_page[:, page - 1:page]
        tiles.append(_dot(qbd, k_refs[r][0, 0].astype(BF16)) - _dot_hi(expand, cum_page - reach))

    m = cq_row + functools.reduce(jnp.maximum, [jnp.max(t, axis=1, keepdims=True) for t in tiles])
    shift = cq_row - m
    p_new = jnp.exp(tiles[0] + shift)
    l = jnp.sum(p_new, axis=1, keepdims=True)
    acc = _dot(p_new.astype(BF16), vn_ref[0].astype(BF16))
    for r in range(n_pages):
        p = jnp.exp(tiles[r + 1] + shift)
        l = l + jnp.sum(p, axis=1, keepdims=True)
        acc = acc + _dot_nt(p.astype(BF16), v_refs[r][0, 0].astype(BF16))
    full = jnp.where(headmask, acc / l, 0.0)
    o_ref[0] = jnp.sum(full.reshape(FOX_HEADS, ls, FOX_WIDTH), axis=0)


def _fox_decode(layer, page_table, qb, k_new, v_new, sm, b_pad, cache_k, cache_v, cache_cum):
    bs, ls, _ = qb.shape
    n_pages = page_table.shape[1]
    page = cache_k.shape[3]

    def page_map(r):
        return lambda b, pt: (layer, pt[b, n_pages - 1 - r], 0, 0)

    per_sample = lambda n: pl.BlockSpec((1, ls, n), lambda b, pt: (b, 0, 0))
    in_specs = [per_sample(FOX_WIDTH), per_sample(FOX_WIDTH), per_sample(FOX_WIDTH), per_sample(LANES),
                pl.BlockSpec((1, LANES), lambda b, pt: (0, 0))]
    in_specs += [pl.BlockSpec((1, 1, FOX_WIDTH, page), page_map(r)) for r in range(n_pages)]
    in_specs += [pl.BlockSpec((1, 1, FOX_WIDTH, page), page_map(r)) for r in range(n_pages)]
    in_specs += [pl.BlockSpec((1, 1, FOX_HEADS, page), page_map(r)) for r in range(n_pages)]
    grid_spec = pltpu.PrefetchScalarGridSpec(
        num_scalar_prefetch=1, grid=(bs,), in_specs=in_specs,
        out_specs=pl.BlockSpec((1, ls, FOX_WIDTH), lambda b, pt: (b, 0, 0)))
    kern = functools.partial(_fox_decode_kernel, n_pages=n_pages, page=page, ls=ls)
    args = [page_table, qb, k_new, v_new, sm, b_pad]
    args += [cache_k] * n_pages + [cache_v] * n_pages + [cache_cum] * n_pages
    return pl.pallas_call(
        kern, grid_spec=grid_spec, out_shape=jax.ShapeDtypeStruct((bs, ls, FOX_WIDTH), F32),
        compiler_params=_params("arbitrary"), name="fox_decode")(*args)


INV_BASE = 8


def _bf(x):
    return x.astype(BF16)


def _mm(a, b):
    return _dot(_bf(a), _bf(b))


def _split3(x):
    hi = _bf(x)
    r1 = x - hi.astype(F32)
    mid = _bf(r1)
    return hi, mid, _bf(r1 - mid.astype(F32))


def _mm_near_f32(a, b):
    ah, al, _ = _split3(a)
    bh, bl, _ = _split3(b)
    return _dot(ah, bh) + (_dot(ah, bl) + _dot(al, bh))


def _gdn_gates(sm, smt, alog_row, dtb_row, alog_col, dtb_col, cs):
    t_rows = sm.shape[0]
    lg = _log2(cs)
    r = _iota((t_rows, t_rows), 0)
    c = _iota((t_rows, t_rows), 1)
    same = (r >> lg) == (c >> lg)
    incl = same & (c <= r)
    strict = same & (c < r)
    base = min(INV_BASE, cs)
    base_blk = strict & ((r >> _log2(base)) == (c >> _log2(base)))
    merges = []
    s = base
    while s < cs:
        ls_ = _log2(s)
        merges.append(((r >> (ls_ + 1)) == (c >> (ls_ + 1))) & (((r >> ls_) & 1) == 1) & (((c >> ls_) & 1) == 0))
        s *= 2
    g_col = -jnp.exp(alog_row) * _softplus(sm + dtb_row)
    g_row = -jnp.exp(alog_col) * _softplus(smt + dtb_col)
    incl_b, same_b, upto_b = _bf(incl), _bf(same), _bf(same & (r <= c))
    gc_col = sum(_dot(incl_b, part) for part in _split3(g_col))
    gl_col = sum(_dot(same_b, part) for part in _split3(g_col))
    gc_row = sum(_dot(part, upto_b) for part in _split3(g_row))
    return dict(incl=incl, strict=strict, eye=(r == c).astype(F32), base_blk=base_blk, merges=merges,
                base=base, gc_col=gc_col, gl_col=gl_col, gc_row=gc_row, beta=jax.nn.sigmoid(sm))


def _each(f, *lists):
    return [f(*args) for args in zip(*lists)]


def _unit_lower_inverses(a_list, gt):
    powers = _each(lambda a: jnp.where(gt["base_blk"], -a, 0.0), a_list)
    invs = _each(lambda n0: gt["eye"] + n0, powers)
    for _ in range(_log2(gt["base"]) - 1):
        powers = _each(lambda p: _mm(p, p), powers)
        invs = _each(lambda inv, p: inv + _mm(inv, p), invs, powers)
    for merge in gt["merges"]:
        invs_b = _each(_bf, invs)
        lower = _each(lambda inv_b, a: _bf(_dot(inv_b, _bf(jnp.where(merge, a, 0.0)))), invs_b, a_list)
        invs = _each(lambda inv, low, inv_b: inv - _dot(low, inv_b), invs, lower, invs_b)
    return invs


def _gdn_tiles(conv, gt):
    hd = GDN_HEAD_DIM
    heads = range(GDN_HEADS)
    act = lambda base, h: _silu(conv(slice(base + h * hd, base + (h + 1) * hd)))
    l2 = lambda x: x * lax.rsqrt(jnp.sum(x * x, axis=-1, keepdims=True) + EPS)
    q = [l2(act(0, h)) * (hd ** -0.5) for h in heads]
    k = [l2(act(GDN_WIDTH, h)) for h in heads]
    v = [act(2 * GDN_WIDTH, h) for h in heads]
    beta = [gt["beta"][:, GB_COL + h:GB_COL + h + 1] for h in heads]
    gc = [gt["gc_col"][:, GA_COL + h:GA_COL + h + 1] for h in heads]
    gl = [gt["gl_col"][:, GA_COL + h:GA_COL + h + 1] for h in heads]
    gc_row = [gt["gc_row"][GA_COL + h:GA_COL + h + 1, :] for h in heads]

    decay = _each(lambda c, r: jnp.exp(jnp.where(gt["incl"], c - r, -jnp.inf)), gc, gc_row)
    kb = _each(_bf, k)
    kk = _each(_dot_nt, kb, kb)
    qk = _each(lambda q_, kb_, d: _dot_nt(_bf(q_), kb_) * d, q, kb, decay)
    a = _each(lambda kk_, d, b: jnp.where(gt["strict"], kk_ * d * b, 0.0), kk, decay, beta)
    egc = _each(jnp.exp, gc)
    rhs = _each(lambda v_, k_, b, e: jnp.concatenate([v_ * b, k_ * (b * e)], axis=1), v, k, beta, egc)
    inv_b = _each(_bf, _unit_lower_inverses(a, gt))
    x = _each(lambda i, r: _dot(i, _bf(r)), inv_b, rhs)
    resid = _each(lambda r, x_, a_: _bf(r - (x_ + _mm_near_f32(a_, x_))), rhs, x, a)
    x = _each(lambda x_, i, r: x_ + _dot(i, r), x, inv_b, resid)
    q_dec = _each(lambda q_, e: q_ * e, q, egc)
    k_dec = _each(lambda k_, l_, c: k_ * jnp.exp(l_ - c), k, gl, gc)
    return [x_[:, :hd] for x_ in x], [x_[:, hd:] for x_ in x], qk, q_dec, k_dec, gl


def _gdn_prompt_kernel(qkv_ref, sm_ref, smt_ref, gg_ref, cw_ref, alog_row, dtb_row, alog_col, dtb_col, nw_ref,
                       go_ref, s_out_ref, xbuf_ref, s_ref, *, cs):
    t = pl.program_id(1)
    t_rows = qkv_ref.shape[0]

    @pl.when(t == 0)
    def _():
        xbuf_ref[0:SUBLANES, :] = jnp.zeros((SUBLANES, GDN_CONV_DIM), F32)
        s_ref[...] = jnp.zeros_like(s_ref)

    xbuf_ref[SUBLANES:, :] = qkv_ref[...]

    def conv(cols):
        acc = xbuf_ref[SUBLANES:SUBLANES + t_rows, cols] * cw_ref[CONV_WIDTH - 1:CONV_WIDTH, cols]
        for w in range(CONV_WIDTH - 1):
            lo = SUBLANES - (CONV_WIDTH - 1) + w
            acc = acc + xbuf_ref[lo:lo + t_rows, cols] * cw_ref[w:w + 1, cols]
        return acc

    gt = _gdn_gates(sm_ref[...], smt_ref[...], alog_row[...], dtb_row[...], alog_col[...], dtb_col[...], cs)
    hd = GDN_HEAD_DIM
    heads = range(GDN_HEADS)
    u_base, k_cum, qk, q_dec, k_dec, gl = _gdn_tiles(conv, gt)
    states = [s_ref[h] for h in heads]
    us = [[] for _ in heads]
    ois = [[] for _ in heads]
    for ci in range(t_rows // cs):
        rows = slice(ci * cs, (ci + 1) * cs)
        sb = _each(_bf, states)
        u = [u_base[h][rows] - _dot(_bf(k_cum[h][rows]), sb[h]) for h in heads]
        for h in heads:
            ois[h].append(_dot(_bf(q_dec[h][rows]), sb[h]))
            us[h].append(u[h])
        states = [states[h] * jnp.exp(gl[h][ci * cs:ci * cs + 1, :]) + _dot_tn(_bf(k_dec[h][rows]), _bf(u[h]))
                  for h in heads]
    for h in heads:
        s_ref[h] = states[h]
        o = jnp.concatenate(ois[h], axis=0) + _dot(_bf(qk[h]), _bf(jnp.concatenate(us[h], axis=0)))
        cols = slice(h * hd, (h + 1) * hd)
        go_ref[:, cols] = _rms(o, nw_ref[...]) * _silu(gg_ref[:, cols])

    xbuf_ref[0:SUBLANES, :] = qkv_ref[t_rows - SUBLANES:, :]

    @pl.when(t == pl.num_programs(1) - 1)
    def _():
        s_out_ref[0] = s_ref[...]


def _gdn_prompt(qkv, sm, smt, gg, cw, alog_row, dtb_row, alog_col, dtb_col, nw, batch, tile):
    m = qkv.shape[0]
    steps = m // batch // tile
    row = lambda n: pl.BlockSpec((tile, n), lambda b, t: (b * steps + t, 0))
    full = lambda a: pl.BlockSpec(a.shape, lambda b, t: (0,) * a.ndim)
    hd = GDN_HEAD_DIM
    return pl.pallas_call(
        functools.partial(_gdn_prompt_kernel, cs=math.gcd(tile, GDN_CHUNK)), grid=(batch, steps),
        in_specs=[row(GDN_CONV_DIM), row(LANES), pl.BlockSpec((SMALL_COLS, tile), lambda b, t: (0, b * steps + t)),
                  row(GDN_WIDTH), full(cw), full(alog_row), full(dtb_row), full(alog_col), full(dtb_col), full(nw)],
        out_specs=(row(GDN_WIDTH), pl.BlockSpec((1, GDN_HEADS, hd, hd), lambda b, t: (b, 0, 0, 0))),
        out_shape=(jax.ShapeDtypeStruct((m, GDN_WIDTH), F32),
                   jax.ShapeDtypeStruct((batch, GDN_HEADS, hd, hd), F32)),
        scratch_shapes=[pltpu.VMEM((tile + SUBLANES, GDN_CONV_DIM), F32), pltpu.VMEM((GDN_HEADS, hd, hd), F32)],
        compiler_params=_params("arbitrary", "arbitrary"), name="gdn_prompt")(
            qkv, sm, smt, gg, cw, alog_row, dtb_row, alog_col, dtb_col, nw)


def _gdn_sample_kernel(qkv_ref, hist_ref, sm_ref, smt_ref, gg_ref, s0_ref, cw_ref, alog_row, dtb_row,
                       alog_col, dtb_col, nw_ref, *rest, ls):
    go_ref, s_out_ref, xbuf_ref = rest[-3:]
    bt = qkv_ref.shape[0]
    xbuf_ref[:, 0:SUBLANES, :] = hist_ref[...]
    xbuf_ref[:, SUBLANES:, :] = qkv_ref[...]

    def conv(cols):
        acc = xbuf_ref[:, SUBLANES:SUBLANES + ls, cols] * cw_ref[CONV_WIDTH - 1:CONV_WIDTH, cols]
        for w in range(CONV_WIDTH - 1):
            lo = SUBLANES - (CONV_WIDTH - 1) + w
            acc = acc + xbuf_ref[:, lo:lo + ls, cols] * cw_ref[w:w + 1, cols]
        return acc.reshape(bt * ls, acc.shape[-1])

    gt = _gdn_gates(sm_ref[...].reshape(bt * ls, LANES), smt_ref[...], alog_row[...], dtb_row[...],
                    alog_col[...], dtb_col[...], ls)
    u_base, k_cum, qk, q_dec, k_dec, gl = _gdn_tiles(conv, gt)
    for h in range(GDN_HEADS):
        us, ois = [], []
        for bi in range(bt):
            rows = slice(bi * ls, (bi + 1) * ls)
            state = s0_ref[0, bi, h]
            u = u_base[h][rows] - _dot(k_cum[h][rows], state)
            ois.append(_dot(q_dec[h][rows], state))
            s_out_ref[0, bi, h] = state * jnp.exp(gl[h][bi * ls:bi * ls + 1, :]) + _dot_tn(k_dec[h][rows], u)
            us.append(u)
        o = jnp.concatenate(ois, axis=0) + _dot(_bf(qk[h]), _bf(jnp.concatenate(us, axis=0)))
        hd = GDN_HEAD_DIM
        cols = slice(h * hd, (h + 1) * hd)
        gate = _silu(gg_ref[:, :, cols].reshape(bt * ls, hd))
        go_ref[:, :, cols] = (_rms(o, nw_ref[...]) * gate).reshape(bt, ls, hd)


def _gdn_sample(layer, qkv, hist, sm, smt, gg, s0_all, cw, alog_row, dtb_row, alog_col, dtb_col, nw, prev_states, bt):
    bs, ls, _ = qkv.shape
    per = lambda n: pl.BlockSpec((bt, ls, n), lambda i: (i, 0, 0))
    full = lambda a: pl.BlockSpec(a.shape, lambda i: (0,) * a.ndim)
    hd = GDN_HEAD_DIM
    st = pl.BlockSpec((1, bt, GDN_HEADS, hd, hd), lambda i: (layer, i, 0, 0, 0))
    args = [qkv, hist, sm, smt, gg, s0_all, cw, alog_row, dtb_row, alog_col, dtb_col, nw]
    in_specs = [per(GDN_CONV_DIM), pl.BlockSpec((bt, SUBLANES, GDN_CONV_DIM), lambda i: (i, 0, 0)), per(LANES),
                pl.BlockSpec((SMALL_COLS, bt * ls), lambda i: (0, i)), per(GDN_WIDTH), st,
                full(cw), full(alog_row), full(dtb_row), full(alog_col), full(dtb_col), full(nw)]
    aliases = {}
    if prev_states is not None:
        aliases = {len(args): 1}
        args.append(prev_states)
        in_specs.append(pl.BlockSpec(memory_space=pl.ANY))
    return pl.pallas_call(
        functools.partial(_gdn_sample_kernel, ls=ls), grid=(bs // bt,), in_specs=in_specs,
        out_specs=(per(GDN_WIDTH), st),
        out_shape=(jax.ShapeDtypeStruct((bs, ls, GDN_WIDTH), F32), jax.ShapeDtypeStruct(s0_all.shape, F32)),
        scratch_shapes=[pltpu.VMEM((bt, SUBLANES + ls, GDN_CONV_DIM), F32)],
        input_output_aliases=aliases, compiler_params=_params("arbitrary"), name="gdn_sample")(*args)


def _norm_matmul_kernel(x_ref, g_ref, w_ref, *o_refs):
    xn = _rms(x_ref[...], g_ref[...]).astype(BF16)
    n = w_ref.shape[1] // len(o_refs)
    for i, o_ref in enumerate(o_refs):
        o_ref[...] = _dot(xn, w_ref[:, i * n:(i + 1) * n])


def _norm_matmul(x, g, w, n_out, tm):
    m = x.shape[0]
    n = w.shape[1] // n_out
    return pl.pallas_call(
        _norm_matmul_kernel, grid=(m // tm,),
        in_specs=[pl.BlockSpec((tm, x.shape[1]), lambda i: (i, 0)), pl.BlockSpec(g.shape, lambda i: (0, 0)),
                  pl.BlockSpec(w.shape, lambda i: (0, 0))],
        out_specs=tuple(pl.BlockSpec((tm, n), lambda i: (i, 0)) for _ in range(n_out)),
        out_shape=tuple(jax.ShapeDtypeStruct((m, n), F32) for _ in range(n_out)),
        compiler_params=_params("arbitrary"), name="norm_matmul")(x, g, w)


def _matmul_residual_kernel(*refs, n_in):
    x_ref, o_ref = refs[0], refs[-1]
    acc = x_ref[...]
    for a_ref, w_ref in zip(refs[1:1 + n_in], refs[1 + n_in:1 + 2 * n_in]):
        acc = acc + _dot(a_ref[...].astype(BF16), w_ref[...])
    o_ref[...] = acc


def _matmul_residual(x, a_list, w_list, tm):
    m, d = x.shape
    n_in = len(a_list)
    in_specs = [pl.BlockSpec((tm, d), lambda i: (i, 0))]
    in_specs += [pl.BlockSpec((tm, a.shape[1]), lambda i: (i, 0)) for a in a_list]
    in_specs += [pl.BlockSpec(w.shape, lambda i: (0, 0)) for w in w_list]
    return pl.pallas_call(
        functools.partial(_matmul_residual_kernel, n_in=n_in), grid=(m // tm,), in_specs=in_specs,
        out_specs=pl.BlockSpec((tm, d), lambda i: (i, 0)), out_shape=jax.ShapeDtypeStruct((m, d), F32),
        compiler_params=_params("arbitrary"), name="matmul_residual")(x, *a_list, *w_list)


def _mem_attn_prompt_kernel(x_ref, g_ref, wq_ref, wo_ref, mk_ref, mv_ref, o_ref):
    x = x_ref[0]
    xn = _rms(x, g_ref[...]).astype(BF16)
    q = _dot(xn, wq_ref[...]).astype(BF16)
    outs = []
    for h in range(MEM_HEADS):
        cols = slice(h * MEM_HEAD_DIM, (h + 1) * MEM_HEAD_DIM)
        s = _dot_nt(q[:, cols], mk_ref[0, :, cols].astype(BF16)) * (MEM_HEAD_DIM ** -0.5)
        p = jnp.exp(s - jnp.max(s, axis=1, keepdims=True))
        o = _dot(p.astype(BF16), mv_ref[0, :, cols].astype(BF16))
        outs.append(o / jnp.sum(p, axis=1, keepdims=True))
    o_ref[0] = x + _dot(jnp.concatenate(outs, axis=1).astype(BF16), wo_ref[...])


def _mem_attn_prompt(x, g, wq, wo, mk, mv, tm):
    b, l, d = x.shape
    n_mem = mk.shape[1]
    full = lambda a: pl.BlockSpec(a.shape, lambda bi, i: (0,) * a.ndim)
    mem_spec = pl.BlockSpec((1, n_mem, MEM_WIDTH), lambda bi, i: (bi, 0, 0))
    x_spec = pl.BlockSpec((1, tm, d), lambda bi, i: (bi, i, 0))
    return pl.pallas_call(
        _mem_attn_prompt_kernel, grid=(b, l // tm),
        in_specs=[x_spec, full(g), full(wq), full(wo), mem_spec, mem_spec],
        out_specs=x_spec, out_shape=jax.ShapeDtypeStruct(x.shape, F32),
        compiler_params=_params("arbitrary", "arbitrary"), name="mem_attn_prompt")(x, g, wq, wo, mk, mv)


def _mem_attn_sample_kernel(q_ref, mk_ref, mv_ref, o_ref):
    bt, ls, _ = q_ref.shape
    rows = MEM_HEADS * ls
    n = mk_ref.shape[2]
    hd = MEM_HEAD_DIM
    own_head = _mod(_iota((rows, n), 1), MEM_HEADS) == _div(_iota((rows, n), 0), ls)
    for bi in range(bt):
        q = q_ref[bi]
        qs = jnp.concatenate([q[:, h * hd:(h + 1) * hd] for h in range(MEM_HEADS)], axis=0)
        s = _dot_nt(qs.astype(BF16), mk_ref[0, bi].astype(BF16)) * (hd ** -0.5)
        s = jnp.where(own_head, s, -jnp.inf)
        p = jnp.exp(s - jnp.max(s, axis=1, keepdims=True))
        o = _dot(p.astype(BF16), mv_ref[0, bi].astype(BF16)) / jnp.sum(p, axis=1, keepdims=True)
        o_ref[bi] = jnp.concatenate([o[h * ls:(h + 1) * ls] for h in range(MEM_HEADS)], axis=1)


def _mem_attn_sample(layer, q, mk_all, mv_all, bt):
    bs, ls, _ = q.shape
    mem_spec = pl.BlockSpec((1, bt) + mk_all.shape[2:], lambda i: (layer, i, 0, 0))
    q_spec = pl.BlockSpec((bt, ls, MEM_WIDTH), lambda i: (i, 0, 0))
    return pl.pallas_call(
        _mem_attn_sample_kernel, grid=(bs // bt,), in_specs=[q_spec, mem_spec, mem_spec],
        out_specs=q_spec, out_shape=jax.ShapeDtypeStruct(q.shape, F32),
        compiler_params=_params("arbitrary"), name="mem_attn_sample")(q, mk_all, mv_all)


def _ffn_kernel(x_ref, g_ref, wa_ref, wu_ref, wo_ref, gf_ref, o_ref, *, chunk, final_norm):
    x = x_ref[...]
    xn = _rms(x, g_ref[...]).astype(BF16)
    acc = x
    for c in range(wa_ref.shape[1] // chunk):
        cols = slice(c * chunk, (c + 1) * chunk)
        a = _dot(xn, wa_ref[:, cols])
        u = _dot(xn, wu_ref[:, cols])
        acc = acc + _dot((_silu(a) * u).astype(BF16), wo_ref[cols, :])
    if final_norm:
        acc = _rms(acc, gf_ref[...])
    o_ref[...] = acc


def _ffn(x, g, wa, wu, wo, gf, final_norm, tm):
    m, d = x.shape
    full = lambda a: pl.BlockSpec(a.shape, lambda i: (0,) * a.ndim)
    row = pl.BlockSpec((tm, d), lambda i: (i, 0))
    return pl.pallas_call(
        functools.partial(_ffn_kernel, chunk=2 * LANES, final_norm=final_norm), grid=(m // tm,),
        in_specs=[row, full(g), full(wa), full(wu), full(wo), full(gf)], out_specs=row,
        out_shape=jax.ShapeDtypeStruct((m, d), F32), compiler_params=_params("arbitrary"), name="ffn")(
            x, g, wa, wu, wo, gf)


def _pick(m, cap):
    t = min(m, cap)
    while m % t:
        t -= SUBLANES
    assert t > 0 and m % t == 0, (m, cap)
    return t


def _pad_lanes(v, offset):
    row = jnp.zeros((1, LANES), F32).at[0, offset:offset + v.shape[0]].set(v)
    return row, row[0, :SMALL_COLS].reshape(SMALL_COLS, 1)


def kernel(x_prompt, x_sample, cache_fox_k, cache_fox_v, cache_fox_logf, state_gdn, state_gdn_conv, cache_mem_k, cache_mem_v, page_table, mem_prompt, g_norm_mix, w_in, b_fox_f, gdn_conv_w, gdn_a_log, gdn_dt_bias, gdn_norm_w, w_out, g_norm_memin, w_mem_kv, g_norm_mem, w_mem_q, w_mem_o, g_norm_ffn, w_ffn_in, w_ffn_out, g_final):
    bp, lp, d = x_prompt.shape
    bs, ls, _ = x_sample.shape
    depth = w_in.shape[0]
    n_phys, page = cache_fox_k.shape[1], cache_fox_k.shape[2]
    n_pages = page_table.shape[1]
    n_mem = mem_prompt.shape[1]
    d_ff = w_ffn_out.shape[1]
    mp, ms = bp * lp, bs * ls
    hd = GDN_HEAD_DIM

    yp = x_prompt.reshape(mp, d)
    ys = x_sample.reshape(ms, d)
    mem_flat = mem_prompt.reshape(bp * n_mem, d)
    cache_k = cache_fox_k.transpose(0, 1, 3, 4, 2).reshape(depth, n_phys, FOX_WIDTH, page)
    cache_v = cache_fox_v.transpose(0, 1, 3, 4, 2).reshape(depth, n_phys, FOX_WIDTH, page)
    page_rows = depth * n_phys * FOX_HEADS
    cache_cum = _cumsum_pages(cache_fox_logf.transpose(0, 1, 3, 2).reshape(page_rows, page), _pick(page_rows, 512))
    cache_cum = cache_cum.reshape(depth, n_phys, FOX_HEADS, page)
    tokens_per_row = LANES // FOX_HEADS
    hist = jnp.pad(state_gdn_conv, ((0, 0), (0, 0), (SUBLANES - (CONV_WIDTH - 1), 0), (0, 0)))
    mem_k_cache = cache_mem_k.reshape(depth, bs, n_mem * MEM_HEADS, MEM_HEAD_DIM)
    mem_v_cache = cache_mem_v.reshape(depth, bs, n_mem * MEM_HEADS, MEM_HEAD_DIM)

    o1 = FOX_WIDTH * 3
    o2 = o1 + FOX_HEADS
    o3 = o2 + GDN_CONV_DIM
    o4 = o3 + 2 * GDN_HEADS
    row2 = lambda v: v.reshape(1, -1)

    outs_p = {k: [] for k in ("lf", "gs", "gc", "mk", "mv")}
    outs_s = {k: [] for k in ("fk", "fv", "lf", "gc")}
    kt_all = vt_all = states_s = None
    for l in range(depth):
        w = w_in[l]
        wq = w[:, :FOX_WIDTH].astype(BF16)
        wqt = w[:, :FOX_WIDTH].T.astype(BF16)
        wkv = w[:, FOX_WIDTH:o1].astype(BF16)
        wk = w[:, FOX_WIDTH:2 * FOX_WIDTH].astype(BF16)
        wkvt = w[:, FOX_WIDTH:o1].T.astype(BF16)
        wg = jnp.concatenate([w[:, o2:o3], w[:, o4:]], axis=1).astype(BF16)
        w_small = jnp.concatenate([w[:, o1:o2], w[:, o3:o4]], axis=1)
        ws = jnp.pad(w_small, ((0, 0), (0, LANES - SMALL_COLS))).astype(BF16)
        wst = w_small.T.astype(BF16)
        b_row = row2(b_fox_f[l])
        b_col = b_fox_f[l].reshape(FOX_HEADS, 1)
        b_pad = jnp.pad(b_row, ((0, 0), (0, LANES - FOX_HEADS)))
        alog_row, alog_col = _pad_lanes(gdn_a_log[l], GA_COL)
        dtb_row, dtb_col = _pad_lanes(gdn_dt_bias[l], GA_COL)
        cw = gdn_conv_w[l]
        nw = row2(gdn_norm_w[l])
        wo_fox = w_out[l][:FOX_WIDTH].astype(BF16)
        wo_gdn = w_out[l][FOX_WIDTH:].astype(BF16)
        w_kv = w_mem_kv[l].astype(BF16)
        w_q = w_mem_q[l].astype(BF16)
        w_o = w_mem_o[l].astype(BF16)
        wa = w_ffn_in[l][:, :d_ff].astype(BF16)
        wu = w_ffn_in[l][:, d_ff:].astype(BF16)
        wf = w_ffn_out[l].astype(BF16)
        g_mix, g_memin, g_mem, g_ffn = row2(g_norm_mix[l]), row2(g_norm_memin[l]), row2(g_norm_mem[l]), row2(g_norm_ffn[l])
        gf = row2(g_final)
        last = l == depth - 1

        kt_all, vt_all, kaug, qtaug, vtaug, qkv, gg, lft, sm, smt = _in_proj_prompt(
            l, depth, bp, yp, g_mix, wqt, wk, wkvt, wg, ws, wst, b_row, b_col, kt_all, vt_all, _pick(lp, 256))
        fo = _fox_prompt(kaug, qtaug, vtaug, _pick(lp, 512)).reshape(mp, FOX_WIDTH)
        go, s_new = _gdn_prompt(qkv, sm, smt, gg, cw, alog_row, dtb_row, alog_col, dtb_col, nw, bp, _pick(lp, 256))
        yp = _matmul_residual(yp, [fo, go], [wo_fox, wo_gdn], _pick(mp, 512))
        mk, mv = _norm_matmul(mem_flat, g_memin, w_kv, 2, _pick(bp * n_mem, 256))
        yp = _mem_attn_prompt(yp.reshape(bp, lp, d), g_mem, w_q, w_o, mk.reshape(bp, n_mem, MEM_WIDTH),
                              mv.reshape(bp, n_mem, MEM_WIDTH), _pick(lp, 512)).reshape(mp, d)
        yp = _ffn(yp, g_ffn, wa, wu, wf, gf, last, _pick(mp, 256))
        outs_p["lf"].append(lft.transpose(0, 2, 1))
        outs_p["gs"].append(s_new)
        outs_p["gc"].append(qkv.reshape(bp, lp, GDN_CONV_DIM)[:, lp - (CONV_WIDTH - 1):])
        outs_p["mk"].append(mk.reshape(bp, n_mem, MEM_HEADS, MEM_HEAD_DIM))
        outs_p["mv"].append(mv.reshape(bp, n_mem, MEM_HEADS, MEM_HEAD_DIM))

        fqb, fk, fv, qkv, gg, lf, sm, smt = _in_proj_sample(ys, g_mix, wq, wkv, wg, ws, wst, b_row, _pick(ms, 256))
        s3 = lambda a: a.reshape(bs, ls, a.shape[-1])
        fo = _fox_decode(l, page_table, s3(fqb), s3(fk), s3(fv), s3(sm), b_pad, cache_k, cache_v,
                         cache_cum).reshape(ms, FOX_WIDTH)
        go, states_s = _gdn_sample(l, s3(qkv), hist[l], s3(sm), smt, s3(gg), state_gdn, cw, alog_row, dtb_row,
                                   alog_col, dtb_col, nw, states_s, _pick(bs, 16))
        ys = _matmul_residual(ys, [fo, go.reshape(ms, GDN_WIDTH)], [wo_fox, wo_gdn], _pick(ms, 512))
        (q_mem,) = _norm_matmul(ys, g_mem, w_q, 1, _pick(ms, 256))
        o_mem = _mem_attn_sample(l, s3(q_mem), mem_k_cache, mem_v_cache, _pick(bs, 8)).reshape(ms, MEM_WIDTH)
        ys = _matmul_residual(ys, [o_mem], [w_o], _pick(ms, 512))
        ys = _ffn(ys, g_ffn, wa, wu, wf, gf, last, _pick(ms, 256))
        outs_s["fk"].append(fk.reshape(bs, ls, FOX_HEADS, FOX_HEAD_DIM))
        outs_s["fv"].append(fv.reshape(bs, ls, FOX_HEADS, FOX_HEAD_DIM))
        outs_s["lf"].append(lf.reshape(bs, ls, FOX_HEADS))
        outs_s["gc"].append(s3(qkv)[:, ls - (CONV_WIDTH - 1):])

    st = jnp.stack
    heads_last = lambda a: a.reshape(depth, bp, FOX_HEADS, FOX_HEAD_DIM, lp).transpose(0, 1, 4, 2, 3)
    return (yp.reshape(bp, lp, d), ys.reshape(bs, ls, d),
            heads_last(kt_all), heads_last(vt_all), st(outs_p["lf"]), st(outs_p["gs"]), st(outs_p["gc"]),
            st(outs_p["mk"]), st(outs_p["mv"]),
            st(outs_s["fk"]), st(outs_s["fv"]), st(outs_s["lf"]), states_s, st(outs_s["gc"]))
```

```python
import functools
import math

import jax
import jax.numpy as jnp
from jax import lax
from jax.experimental import pallas as pl
from jax.experimental.pallas import tpu as pltpu

F32 = jnp.float32
BF16 = jnp.bfloat16
HI = lax.Precision.HIGHEST

EPS = 1e-6
D_MODEL = 1024
FOX_HEADS = 8
FOX_HEAD_DIM = 64
FOX_WIDTH = FOX_HEADS * FOX_HEAD_DIM
GDN_HEADS = 4
GDN_HEAD_DIM = 128
GDN_WIDTH = GDN_HEADS * GDN_HEAD_DIM
GDN_CONV_DIM = 3 * GDN_WIDTH
CONV_WIDTH = 4
GDN_CHUNK = 64
MEM_HEADS = 4
MEM_HEAD_DIM = 128
MEM_WIDTH = MEM_HEADS * MEM_HEAD_DIM
LANES = 128
SUBLANES = 8
SMALL_COLS = 16
GA_COL = FOX_HEADS
GB_COL = FOX_HEADS + GDN_HEADS
VMEM_LIMIT = 56 * 1024 * 1024


def _params(*sem):
    return pltpu.CompilerParams(dimension_semantics=sem, vmem_limit_bytes=VMEM_LIMIT)


def _dot(a, b):
    return jnp.dot(a, b, preferred_element_type=F32)


def _dot_nt(a, b):
    return lax.dot_general(a, b, (((1,), (1,)), ((), ())), preferred_element_type=F32)


def _dot_tn(a, b):
    return lax.dot_general(a, b, (((0,), (0,)), ((), ())), preferred_element_type=F32)


def _dot_hi(a, b):
    return jnp.dot(a, b, precision=HI, preferred_element_type=F32)


def _dot_nt_hi(a, b):
    return lax.dot_general(a, b, (((1,), (1,)), ((), ())), precision=HI, preferred_element_type=F32)


def _rms(x, g):
    return x * lax.rsqrt(jnp.mean(x * x, axis=-1, keepdims=True) + EPS) * g


def _log_sigmoid(x):
    return jnp.minimum(x, 0.0) - jnp.log1p(jnp.exp(-jnp.abs(x)))


def _softplus(x):
    return jnp.maximum(x, 0.0) + jnp.log1p(jnp.exp(-jnp.abs(x)))


def _silu(x):
    return x * jax.nn.sigmoid(x)


def _iota(shape, dim):
    return lax.broadcasted_iota(jnp.int32, shape, dim)


def _log2(n):
    lg = n.bit_length() - 1
    assert n == 1 << lg, n
    return lg


def _div(x, n):
    return x >> _log2(n)


def _mod(x, n):
    _log2(n)
    return x & (n - 1)


def _in_proj_common(x_ref, g_ref, wq_ref, wg_ref, ws_ref, wst_ref, b_ref,
                    fqb_ref, qkv_ref, gg_ref, lf_ref, sm_ref, smt_ref):
    xn = _rms(x_ref[...], g_ref[...]).astype(BF16)
    fqb_ref[...] = (_dot(xn, wq_ref[...]) * (FOX_HEAD_DIM ** -0.5)).astype(BF16)
    qkv_ref[...] = _dot(xn, wg_ref[:, 0:GDN_CONV_DIM])
    gg_ref[...] = _dot(xn, wg_ref[:, GDN_CONV_DIM:])
    sm = _dot(xn, ws_ref[...])
    smt = _dot_nt(wst_ref[...], xn)
    sm_ref[...] = sm
    smt_ref[...] = smt
    lf_ref[...] = _log_sigmoid(sm[:, 0:FOX_HEADS] + b_ref[...])
    return xn, smt


LOG2E = 1.4426950408889634
AUG_ROWS = SUBLANES


def _in_proj_prompt_kernel(x_ref, g_ref, wqt_ref, wk_ref, wkvt_ref, wg_ref, ws_ref, wst_ref, b_ref, bcol_ref,
                           *rest, tiles_per_seq):
    (kt_ref, vt_ref, kaug_ref, qtaug_ref, vtaug_ref, qkv_ref, gg_ref, lft_ref, sm_ref, smt_ref,
     crow_ref, ccol_ref) = rest[-12:]
    tm = x_ref.shape[0]
    hd = FOX_HEAD_DIM

    @pl.when(pl.program_id(0) % tiles_per_seq == 0)
    def _():
        crow_ref[...] = jnp.zeros_like(crow_ref)
        ccol_ref[...] = jnp.zeros_like(ccol_ref)

    xn = _rms(x_ref[...], g_ref[...]).astype(BF16)
    sm = _dot(xn, ws_ref[...])
    smt = _dot_nt(wst_ref[...], xn)
    sm_ref[...] = sm
    smt_ref[...] = smt
    lf_col = _log_sigmoid(sm[:, 0:FOX_HEADS] + b_ref[...])
    lf_row = _log_sigmoid(smt[0:FOX_HEADS, :] + bcol_ref[...])
    lft_ref[0] = lf_row
    kt = _dot_nt(wkvt_ref[0:FOX_WIDTH, :], xn)
    kt_ref[0, 0] = kt
    vt = _dot_nt(wkvt_ref[FOX_WIDTH:, :], xn)
    vt_ref[0, 0] = vt
    k_tok = _dot(xn, wk_ref[...])
    qt = _dot_nt(wqt_ref[...], xn) * (hd ** -0.5 * LOG2E)

    r = _iota((tm, tm), 0)
    c = _iota((tm, tm), 1)
    c_col = sum(_dot(_bf(c <= r), part) for part in _split3(lf_col)) + crow_ref[...]
    c_row = sum(_dot(part, _bf(r <= c)) for part in _split3(lf_row)) + ccol_ref[...]
    crow_ref[...] = c_col[tm - 1:tm, :]
    ccol_ref[...] = c_row[:, tm - 1:tm]

    lane = _iota((tm, LANES), 1)
    sub = _iota((AUG_ROWS, tm), 0)
    ones_key = jnp.where((lane >= hd + 3) & (lane < hd + 6), 1.0, 0.0)
    zeros_tail = jnp.zeros((LANES - hd - AUG_ROWS, tm), F32)
    value_tail = jnp.where(sub == 0, 1.0, 0.0)
    gdn_cols = wg_ref.shape[1] // FOX_HEADS
    for h in range(FOX_HEADS):
        cols = slice(h * gdn_cols, (h + 1) * gdn_cols)
        part = _dot(xn, wg_ref[:, cols])
        if (h + 1) * gdn_cols <= GDN_CONV_DIM:
            qkv_ref[:, cols] = part
        else:
            gg_ref[:, h * gdn_cols - GDN_CONV_DIM:(h + 1) * gdn_cols - GDN_CONV_DIM] = part
        pair = k_tok[:, (h // 2) * LANES:(h // 2 + 1) * LANES]
        k_h = pair if h % 2 == 0 else pltpu.roll(pair, hd, 1)
        k1, k2, k3 = [p.astype(F32) for p in _split3(c_col[:, h:h + 1] * (-LOG2E))]
        extra = jnp.where(lane == hd, k1, jnp.where(lane == hd + 1, k2, jnp.where(lane == hd + 2, k3, ones_key)))
        kaug_ref[0, h] = jnp.where(lane < hd, k_h, extra).astype(BF16)
        q1, q2, q3 = [p.astype(F32) for p in _split3(c_row[h:h + 1, :] * LOG2E)]
        query_tail = jnp.where(sub < 3, 1.0, jnp.where(sub == 3, q1, jnp.where(sub == 4, q2,
                                                                           jnp.where(sub == 5, q3, 0.0))))
        rows = slice(h * hd, (h + 1) * hd)
        qtaug_ref[0, h] = jnp.concatenate([qt[rows], query_tail, zeros_tail], axis=0).astype(BF16)
        vtaug_ref[0, h] = jnp.concatenate([vt[rows], value_tail, zeros_tail], axis=0).astype(BF16)


def _in_proj_prompt(layer, depth, batch, x, g, wqt, wk, wkvt, wg, ws, wst, b_row, b_col, prev_kt, prev_vt, tm):
    m = x.shape[0]
    l = m // batch
    nt = l // tm
    row = lambda n: pl.BlockSpec((tm, n), lambda i: (i, 0))
    full = lambda a: pl.BlockSpec(a.shape, lambda i: (0,) * a.ndim)
    kv_all = pl.BlockSpec((1, 1, FOX_WIDTH, tm), lambda i: (layer, i // nt, 0, i % nt))
    feat_major = pl.BlockSpec((1, FOX_HEADS, LANES, tm), lambda i: (i // nt, 0, 0, i % nt))
    out_shape = (
        jax.ShapeDtypeStruct((depth, batch, FOX_WIDTH, l), F32),
        jax.ShapeDtypeStruct((depth, batch, FOX_WIDTH, l), F32),
        jax.ShapeDtypeStruct((batch, FOX_HEADS, l, LANES), BF16),
        jax.ShapeDtypeStruct((batch, FOX_HEADS, LANES, l), BF16),
        jax.ShapeDtypeStruct((batch, FOX_HEADS, LANES, l), BF16),
        jax.ShapeDtypeStruct((m, GDN_CONV_DIM), F32),
        jax.ShapeDtypeStruct((m, GDN_WIDTH), F32),
        jax.ShapeDtypeStruct((batch, FOX_HEADS, l), F32),
        jax.ShapeDtypeStruct((m, LANES), F32),
        jax.ShapeDtypeStruct((SMALL_COLS, m), F32),
    )
    out_specs = (kv_all, kv_all,
                 pl.BlockSpec((1, FOX_HEADS, tm, LANES), lambda i: (i // nt, 0, i % nt, 0)), feat_major, feat_major,
                 row(GDN_CONV_DIM), row(GDN_WIDTH),
                 pl.BlockSpec((1, FOX_HEADS, tm), lambda i: (i // nt, 0, i % nt)), row(LANES),
                 pl.BlockSpec((SMALL_COLS, tm), lambda i: (0, i)))
    args = [x, g, wqt, wk, wkvt, wg, ws, wst, b_row, b_col]
    in_specs = [row(D_MODEL)] + [full(a) for a in args[1:]]
    aliases = {}
    if prev_kt is not None:
        aliases = {len(args): 0, len(args) + 1: 1}
        args += [prev_kt, prev_vt]
        in_specs += [pl.BlockSpec(memory_space=pl.ANY)] * 2
    return pl.pallas_call(
        functools.partial(_in_proj_prompt_kernel, tiles_per_seq=nt), grid=(m // tm,), in_specs=in_specs,
        out_specs=out_specs, out_shape=out_shape, input_output_aliases=aliases,
        scratch_shapes=[pltpu.VMEM((1, FOX_HEADS), F32), pltpu.VMEM((FOX_HEADS, 1), F32)],
        compiler_params=_params("arbitrary"), name="in_proj_prompt")(*args)


def _in_proj_sample_kernel(x_ref, g_ref, wq_ref, wkv_ref, wg_ref, ws_ref, wst_ref, b_ref,
                           fqb_ref, fk_ref, fv_ref, qkv_ref, gg_ref, lf_ref, sm_ref, smt_ref):
    xn, _ = _in_proj_common(x_ref, g_ref, wq_ref, wg_ref, ws_ref, wst_ref, b_ref,
                            fqb_ref, qkv_ref, gg_ref, lf_ref, sm_ref, smt_ref)
    fk_ref[...] = _dot(xn, wkv_ref[:, 0:FOX_WIDTH])
    fv_ref[...] = _dot(xn, wkv_ref[:, FOX_WIDTH:])


def _in_proj_sample(x, g, wq, wkv, wg, ws, wst, b_row, tm):
    m = x.shape[0]
    row = lambda n: pl.BlockSpec((tm, n), lambda i: (i, 0))
    full = lambda a: pl.BlockSpec(a.shape, lambda i: (0,) * a.ndim)
    out_shape = (
        jax.ShapeDtypeStruct((m, FOX_WIDTH), BF16),
        jax.ShapeDtypeStruct((m, FOX_WIDTH), F32),
        jax.ShapeDtypeStruct((m, FOX_WIDTH), F32),
        jax.ShapeDtypeStruct((m, GDN_CONV_DIM), F32),
        jax.ShapeDtypeStruct((m, GDN_WIDTH), F32),
        jax.ShapeDtypeStruct((m, FOX_HEADS), F32),
        jax.ShapeDtypeStruct((m, LANES), F32),
        jax.ShapeDtypeStruct((SMALL_COLS, m), F32),
    )
    out_specs = (row(FOX_WIDTH), row(FOX_WIDTH), row(FOX_WIDTH), row(GDN_CONV_DIM), row(GDN_WIDTH),
                 row(FOX_HEADS), row(LANES), pl.BlockSpec((SMALL_COLS, tm), lambda i: (0, i)))
    args = [x, g, wq, wkv, wg, ws, wst, b_row]
    return pl.pallas_call(
        _in_proj_sample_kernel, grid=(m // tm,), in_specs=[row(D_MODEL)] + [full(a) for a in args[1:]],
        out_specs=out_specs, out_shape=out_shape, compiler_params=_params("arbitrary"),
        name="in_proj_sample")(*args)


def _cumsum_pages_kernel(x_ref, o_ref):
    n = x_ref.shape[1]
    upto = (_iota((n, n), 0) <= _iota((n, n), 1)).astype(F32)
    o_ref[...] = _dot_hi(x_ref[...], upto)


def _cumsum_pages(x, block_rows):
    rows, n = x.shape
    spec = pl.BlockSpec((block_rows, n), lambda i: (i, 0))
    return pl.pallas_call(
        _cumsum_pages_kernel, grid=(rows // block_rows,), in_specs=[spec], out_specs=spec,
        out_shape=jax.ShapeDtypeStruct(x.shape, F32), compiler_params=_params("arbitrary"),
        name="cumsum_pages")(x)


FOX_UNIT_Q = 256
FOX_UNIT_GROUP = 4


def _fox_prompt_kernel(qi_ref, ki_ref, k_ref, qt_ref, vt_ref, o_ref, m_ref, acc_ref, *, tq, tk):
    step_id = pl.program_id(1)
    qi = qi_ref[step_id]
    ki = ki_ref[step_id]
    uq = min(FOX_UNIT_Q, tq)
    hd = FOX_HEAD_DIM

    @pl.when(ki == 0)
    def _():
        m_ref[...] = jnp.full_like(m_ref, -jnp.inf)
        acc_ref[...] = jnp.zeros_like(acc_ref)

    units = [(h, qc) for h in range(FOX_HEADS) for qc in range(tq // uq)]

    def step(diagonal):
        for g0 in range(0, len(units), FOX_UNIT_GROUP):
            group = units[g0:g0 + FOX_UNIT_GROUP]
            cols = [slice(qc * uq, (qc + 1) * uq) for _, qc in group]
            keys = [slice(0, (qc + 1) * uq if diagonal else tk) for _, qc in group]
            ts = [_dot(k_ref[0, h, ks], qt_ref[0, h, :, cs_]) for (h, _), cs_, ks in zip(group, cols, keys)]
            if diagonal:
                ts = [jnp.where(_iota(t.shape, 0) <= _iota(t.shape, 1) + qc * uq, t, -jnp.inf)
                      for t, (_, qc) in zip(ts, group)]
            m_old = [m_ref[h, :, cs_] for (h, _), cs_ in zip(group, cols)]
            m_new = [jnp.maximum(mo, jnp.max(t, axis=0, keepdims=True)) for mo, t in zip(m_old, ts)]
            ps = [jnp.exp2(t - mn).astype(BF16) for t, mn in zip(ts, m_new)]
            pvs = [_dot(vt_ref[0, h, :, ks], p) for (h, _), ks, p in zip(group, keys, ps)]
            for (h, _), cs_, mo, mn, pv in zip(group, cols, m_old, m_new, pvs):
                acc_ref[h, :, cs_] = jnp.exp2(mo - mn) * acc_ref[h, :, cs_] + pv
                m_ref[h, :, cs_] = mn

    @pl.when(ki < qi)
    def _():
        step(False)

    @pl.when(ki == qi)
    def _():
        step(True)
        for j in range(FOX_HEADS // 2):
            pair = [acc_ref[h, 0:hd, :] / acc_ref[h, hd:hd + 1, :] for h in (2 * j, 2 * j + 1)]
            o_ref[0, :, j * LANES:(j + 1) * LANES] = jnp.concatenate(pair, axis=0).T


def _fox_prompt(kaug, qtaug, vtaug, tq):
    b, _, l, _ = kaug.shape
    tk = tq
    n = l // tq
    pairs = [(qi, ki) for qi in range(n) for ki in range(qi + 1)]
    qi_tab = jnp.asarray([p[0] for p in pairs], jnp.int32)
    ki_tab = jnp.asarray([p[1] for p in pairs], jnp.int32)
    grid_spec = pltpu.PrefetchScalarGridSpec(
        num_scalar_prefetch=2, grid=(b, len(pairs)),
        in_specs=[pl.BlockSpec((1, FOX_HEADS, tk, LANES), lambda bi, s, qt, kt: (bi, 0, kt[s], 0)),
                  pl.BlockSpec((1, FOX_HEADS, LANES, tq), lambda bi, s, qt, kt: (bi, 0, 0, qt[s])),
                  pl.BlockSpec((1, FOX_HEADS, LANES, tk), lambda bi, s, qt, kt: (bi, 0, 0, kt[s]))],
        out_specs=pl.BlockSpec((1, tq, FOX_WIDTH), lambda bi, s, qt, kt: (bi, qt[s], 0)),
        scratch_shapes=[pltpu.VMEM((FOX_HEADS, 1, tq), F32), pltpu.VMEM((FOX_HEADS, LANES, tq), F32)])
    return pl.pallas_call(
        functools.partial(_fox_prompt_kernel, tq=tq, tk=tk), grid_spec=grid_spec,
        out_shape=jax.ShapeDtypeStruct((b, l, FOX_WIDTH), F32),
        compiler_params=_params("arbitrary", "arbitrary"), name="fox_prompt")(qi_tab, ki_tab, kaug, qtaug, vtaug)


def _fox_decode_kernel(pt_ref, q_ref, kn_ref, vn_ref, sm_ref, b_ref, *rest, n_pages, page, ls):
    k_refs, v_refs = rest[0:n_pages], rest[n_pages:2 * n_pages]
    cum_ref, o_ref = rest[2 * n_pages:]
    sample = pl.program_id(0)
    rows = FOX_HEADS * ls
    headmask = _div(_iota((rows, FOX_WIDTH), 0), ls) == _div(_iota((rows, FOX_WIDTH), 1), FOX_HEAD_DIM)
    expand = (_div(_iota((rows, FOX_HEADS), 0), ls) == _iota((rows, FOX_HEADS), 1)).astype(F32)
    q = q_ref[0].astype(F32)
    qbd = jnp.where(headmask, jnp.concatenate([q] * FOX_HEADS, axis=0), 0.0).astype(BF16)

    lane = _iota((ls, LANES), 1)
    lf_new = jnp.where(lane < FOX_HEADS, _log_sigmoid(sm_ref[0] + b_ref[...]), 0.0)
    tok = _iota((ls, LANES), 0)
    cum_new = lf_new
    for s in range(1, ls):
        cum_new = cum_new + jnp.where(tok >= s, pltpu.roll(lf_new, s, 0), 0.0)
    cum_new = cum_new[:, 0:FOX_HEADS]
    query_of_row = (_mod(_iota((rows, ls), 0), ls) == _iota((rows, ls), 1)).astype(F32)
    cq_row = jnp.sum(_dot_hi(query_of_row, cum_new) * expand, axis=1, keepdims=True)

    vis = _iota((rows, ls), 1) <= _mod(_iota((rows, ls), 0), ls)
    tiles = [jnp.where(vis, _dot_nt(qbd, kn_ref[0].astype(BF16)) - _dot_nt_hi(expand, cum_new), -jnp.inf)]
    reach = jnp.zeros((FOX_HEADS, 1), F32)
    key_gate = []
    for r in range(n_pages):
        cum_page = cum_ref[0, pt_ref[sample, n_pages - 1 - r]]
        reach = reach + cum_page[:, page - 1:page]
        key_gate.append(cum_page - reach)
    key_gate = _dot_hi(expand, jnp.concatenate(key_gate, axis=1))
    for r in range(n_pages):
        tiles.append(_dot(qbd, k_refs[r][0, 0].astype(BF16)) - key_gate[:, r * page:(r + 1) * page])

    m = cq_row + functools.reduce(jnp.maximum, [jnp.max(t, axis=1, keepdims=True) for t in tiles])
    shift = cq_row - m
    p_new = jnp.exp(tiles[0] + shift)
    l = jnp.sum(p_new, axis=1, keepdims=True)
    acc = _dot(p_new.astype(BF16), vn_ref[0].astype(BF16))
    for r in range(n_pages):
        p = jnp.exp(tiles[r + 1] + shift)
        l = l + jnp.sum(p, axis=1, keepdims=True)
        acc = acc + _dot_nt(p.astype(BF16), v_refs[r][0, 0].astype(BF16))
    full = jnp.where(headmask, acc / l, 0.0)
    o_ref[0] = jnp.sum(full.reshape(FOX_HEADS, ls, FOX_WIDTH), axis=0)


def _fox_decode(layer, page_table, qb, k_new, v_new, sm, b_pad, cache_k, cache_v, cache_cum):
    bs, ls, _ = qb.shape
    n_pages = page_table.shape[1]
    page = cache_k.shape[3]

    def page_map(r):
        return lambda b, pt: (layer, pt[b, n_pages - 1 - r], 0, 0)

    per_sample = lambda n: pl.BlockSpec((1, ls, n), lambda b, pt: (b, 0, 0))
    in_specs = [per_sample(FOX_WIDTH), per_sample(FOX_WIDTH), per_sample(FOX_WIDTH), per_sample(LANES),
                pl.BlockSpec((1, LANES), lambda b, pt: (0, 0))]
    in_specs += [pl.BlockSpec((1, 1, FOX_WIDTH, page), page_map(r)) for r in range(n_pages)]
    in_specs += [pl.BlockSpec((1, 1, FOX_WIDTH, page), page_map(r)) for r in range(n_pages)]
    in_specs += [pl.BlockSpec((1,) + cache_cum.shape[1:], lambda b, pt: (layer, 0, 0, 0))]
    grid_spec = pltpu.PrefetchScalarGridSpec(
        num_scalar_prefetch=1, grid=(bs,), in_specs=in_specs,
        out_specs=pl.BlockSpec((1, ls, FOX_WIDTH), lambda b, pt: (b, 0, 0)))
    kern = functools.partial(_fox_decode_kernel, n_pages=n_pages, page=page, ls=ls)
    args = [page_table, qb, k_new, v_new, sm, b_pad]
    args += [cache_k] * n_pages + [cache_v] * n_pages + [cache_cum]
    return pl.pallas_call(
        kern, grid_spec=grid_spec, out_shape=jax.ShapeDtypeStruct((bs, ls, FOX_WIDTH), F32),
        compiler_params=_params("arbitrary"), name="fox_decode")(*args)


INV_BASE = 8
GDN_SUB_TILE = 128


def _bf(x):
    return x.astype(BF16)


def _mm(a, b):
    return _dot(_bf(a), _bf(b))


def _split3(x):
    hi = _bf(x)
    r1 = x - hi.astype(F32)
    mid = _bf(r1)
    return hi, mid, _bf(r1 - mid.astype(F32))


def _mm_near_f32(a, b):
    ah, al, _ = _split3(a)
    bh, bl, _ = _split3(b)
    return _dot(ah, bh) + (_dot(ah, bl) + _dot(al, bh))


def _gdn_gates(sm, smt, alog_row, dtb_row, alog_col, dtb_col, cs, sub):
    t_rows = sm.shape[0]
    lg = _log2(cs)
    r = _iota((sub, sub), 0)
    c = _iota((sub, sub), 1)
    same = (r >> lg) == (c >> lg)
    incl = same & (c <= r)
    strict = same & (c < r)
    base = min(INV_BASE, cs)
    base_blk = strict & ((r >> _log2(base)) == (c >> _log2(base)))
    merges = []
    s = base
    while s < cs:
        ls_ = _log2(s)
        merges.append(((r >> (ls_ + 1)) == (c >> (ls_ + 1))) & (((r >> ls_) & 1) == 1) & (((c >> ls_) & 1) == 0))
        s *= 2
    g_col = -jnp.exp(alog_row) * _softplus(sm + dtb_row)
    g_row = -jnp.exp(alog_col) * _softplus(smt + dtb_col)
    incl_b, same_b, upto_b = _bf(incl), _bf(same), _bf(same & (r <= c))
    beta = jax.nn.sigmoid(sm)
    subs = [slice(i, i + sub) for i in range(0, t_rows, sub)]
    gc_col = [sum(_dot(incl_b, part) for part in _split3(g_col[rows])) for rows in subs]
    gl_col = [sum(_dot(same_b, part) for part in _split3(g_col[rows])) for rows in subs]
    gc_row = [sum(_dot(part, upto_b) for part in _split3(g_row[:, rows])) for rows in subs]
    return dict(incl=incl, strict=strict, eye=(r == c).astype(F32), base_blk=base_blk, merges=merges, base=base,
                subs=subs, gc_col=gc_col, gl_col=gl_col, gc_row=gc_row, beta=[beta[rows] for rows in subs])


def _each(f, *lists):
    return [f(*args) for args in zip(*lists)]


def _unit_lower_inverses(a_list, gt):
    powers = _each(lambda a: jnp.where(gt["base_blk"], -a, 0.0), a_list)
    invs = _each(lambda n0: gt["eye"] + n0, powers)
    for _ in range(_log2(gt["base"]) - 1):
        powers = _each(lambda p: _mm(p, p), powers)
        invs = _each(lambda inv, p: inv + _mm(inv, p), invs, powers)
    for merge in gt["merges"]:
        invs_b = _each(_bf, invs)
        lower = _each(lambda inv_b, a: _bf(_dot(inv_b, _bf(jnp.where(merge, a, 0.0)))), invs_b, a_list)
        invs = _each(lambda inv, low, inv_b: inv - _dot(low, inv_b), invs, lower, invs_b)
    return invs


def _gdn_tiles(conv, gt):
    hd = GDN_HEAD_DIM
    heads = range(GDN_HEADS)
    act = lambda base, h: _silu(conv(slice(base + h * hd, base + (h + 1) * hd)))
    l2 = lambda x: x * lax.rsqrt(jnp.sum(x * x, axis=-1, keepdims=True) + EPS)
    q_full = [l2(act(0, h)) * (hd ** -0.5) for h in heads]
    k_full = [l2(act(GDN_WIDTH, h)) for h in heads]
    v_full = [act(2 * GDN_WIDTH, h) for h in heads]
    units = [(s, h) for s in range(len(gt["subs"])) for h in heads]
    q = [q_full[h][gt["subs"][s]] for s, h in units]
    k = [k_full[h][gt["subs"][s]] for s, h in units]
    v = [v_full[h][gt["subs"][s]] for s, h in units]
    beta = [gt["beta"][s][:, GB_COL + h:GB_COL + h + 1] for s, h in units]
    gc = [gt["gc_col"][s][:, GA_COL + h:GA_COL + h + 1] for s, h in units]
    gl = [gt["gl_col"][s][:, GA_COL + h:GA_COL + h + 1] for s, h in units]
    gc_row = [gt["gc_row"][s][GA_COL + h:GA_COL + h + 1, :] for s, h in units]

    decay = _each(lambda c, r: jnp.exp(jnp.where(gt["incl"], c - r, -jnp.inf)), gc, gc_row)
    kb = _each(_bf, k)
    kk = _each(_dot_nt, kb, kb)
    qk = _each(lambda q_, kb_, d: _dot_nt(_bf(q_), kb_) * d, q, kb, decay)
    a = _each(lambda kk_, d, b: jnp.where(gt["strict"], kk_ * d * b, 0.0), kk, decay, beta)
    egc = _each(jnp.exp, gc)
    rhs = _each(lambda v_, k_, b, e: jnp.concatenate([v_ * b, k_ * (b * e)], axis=1), v, k, beta, egc)
    inv_b = _each(_bf, _unit_lower_inverses(a, gt))
    x = _each(lambda i, r: _dot(i, _bf(r)), inv_b, rhs)
    resid = _each(lambda r, x_, a_: _bf(r - (x_ + _mm_near_f32(a_, x_))), rhs, x, a)
    x = _each(lambda x_, i, r: x_ + _dot(i, r), x, inv_b, resid)
    q_dec = _each(lambda q_, e: q_ * e, q, egc)
    k_dec = _each(lambda k_, l_, c: k_ * jnp.exp(l_ - c), k, gl, gc)
    return [x_[:, :hd] for x_ in x], [x_[:, hd:] for x_ in x], qk, q_dec, k_dec, gl


def _gdn_prompt_kernel(qkv_ref, sm_ref, smt_ref, gg_ref, cw_ref, alog_row, dtb_row, alog_col, dtb_col, nw_ref,
                       go_ref, s_out_ref, xbuf_ref, s_ref, *, cs):
    t = pl.program_id(1)
    t_rows = qkv_ref.shape[0]

    @pl.when(t == 0)
    def _():
        xbuf_ref[0:SUBLANES, :] = jnp.zeros((SUBLANES, GDN_CONV_DIM), F32)
        s_ref[...] = jnp.zeros_like(s_ref)

    xbuf_ref[SUBLANES:, :] = qkv_ref[...]

    def conv(cols):
        acc = xbuf_ref[SUBLANES:SUBLANES + t_rows, cols] * cw_ref[CONV_WIDTH - 1:CONV_WIDTH, cols]
        for w in range(CONV_WIDTH - 1):
            lo = SUBLANES - (CONV_WIDTH - 1) + w
            acc = acc + xbuf_ref[lo:lo + t_rows, cols] * cw_ref[w:w + 1, cols]
        return acc

    sub = min(GDN_SUB_TILE, t_rows)
    gt = _gdn_gates(sm_ref[...], smt_ref[...], alog_row[...], dtb_row[...], alog_col[...], dtb_col[...], cs, sub)
    hd = GDN_HEAD_DIM
    heads = range(GDN_HEADS)
    u_base, k_cum, qk, q_dec, k_dec, gl = _gdn_tiles(conv, gt)
    states = [s_ref[h] for h in heads]
    for s in range(t_rows // sub):
        unit = lambda h: s * GDN_HEADS + h
        us = [[] for _ in heads]
        ois = [[] for _ in heads]
        for ci in range(sub // cs):
            rows = slice(ci * cs, (ci + 1) * cs)
            sb = _each(_bf, states)
            u = [u_base[unit(h)][rows] - _dot(_bf(k_cum[unit(h)][rows]), sb[h]) for h in heads]
            for h in heads:
                ois[h].append(_dot(_bf(q_dec[unit(h)][rows]), sb[h]))
                us[h].append(u[h])
            states = [states[h] * jnp.exp(gl[unit(h)][ci * cs:ci * cs + 1, :])
                      + _dot_tn(_bf(k_dec[unit(h)][rows]), _bf(u[h])) for h in heads]
        for h in heads:
            o = jnp.concatenate(ois[h], axis=0) + _dot(_bf(qk[unit(h)]), _bf(jnp.concatenate(us[h], axis=0)))
            cols = slice(h * hd, (h + 1) * hd)
            go_ref[gt["subs"][s], cols] = _rms(o, nw_ref[...]) * _silu(gg_ref[gt["subs"][s], cols])
    for h in heads:
        s_ref[h] = states[h]

    xbuf_ref[0:SUBLANES, :] = qkv_ref[t_rows - SUBLANES:, :]

    @pl.when(t == pl.num_programs(1) - 1)
    def _():
        s_out_ref[0] = s_ref[...]


def _gdn_prompt(qkv, sm, smt, gg, cw, alog_row, dtb_row, alog_col, dtb_col, nw, batch, tile):
    m = qkv.shape[0]
    steps = m // batch // tile
    row = lambda n: pl.BlockSpec((tile, n), lambda b, t: (b * steps + t, 0))
    full = lambda a: pl.BlockSpec(a.shape, lambda b, t: (0,) * a.ndim)
    hd = GDN_HEAD_DIM
    return pl.pallas_call(
        functools.partial(_gdn_prompt_kernel, cs=math.gcd(tile, GDN_CHUNK)), grid=(batch, steps),
        in_specs=[row(GDN_CONV_DIM), row(LANES), pl.BlockSpec((SMALL_COLS, tile), lambda b, t: (0, b * steps + t)),
                  row(GDN_WIDTH), full(cw), full(alog_row), full(dtb_row), full(alog_col), full(dtb_col), full(nw)],
        out_specs=(row(GDN_WIDTH), pl.BlockSpec((1, GDN_HEADS, hd, hd), lambda b, t: (b, 0, 0, 0))),
        out_shape=(jax.ShapeDtypeStruct((m, GDN_WIDTH), F32),
                   jax.ShapeDtypeStruct((batch, GDN_HEADS, hd, hd), F32)),
        scratch_shapes=[pltpu.VMEM((tile + SUBLANES, GDN_CONV_DIM), F32), pltpu.VMEM((GDN_HEADS, hd, hd), F32)],
        compiler_params=_params("arbitrary", "arbitrary"), name="gdn_prompt")(
            qkv, sm, smt, gg, cw, alog_row, dtb_row, alog_col, dtb_col, nw)


def _gdn_sample_kernel(qkv_ref, hist_ref, sm_ref, smt_ref, gg_ref, s0_ref, cw_ref, alog_row, dtb_row,
                       alog_col, dtb_col, nw_ref, *rest, ls):
    go_ref, s_out_ref, xbuf_ref = rest[-3:]
    bt = qkv_ref.shape[0]
    xbuf_ref[:, 0:SUBLANES, :] = hist_ref[...]
    xbuf_ref[:, SUBLANES:, :] = qkv_ref[...]

    def conv(cols):
        acc = xbuf_ref[:, SUBLANES:SUBLANES + ls, cols] * cw_ref[CONV_WIDTH - 1:CONV_WIDTH, cols]
        for w in range(CONV_WIDTH - 1):
            lo = SUBLANES - (CONV_WIDTH - 1) + w
            acc = acc + xbuf_ref[:, lo:lo + ls, cols] * cw_ref[w:w + 1, cols]
        return acc.reshape(bt * ls, acc.shape[-1])

    gt = _gdn_gates(sm_ref[...].reshape(bt * ls, LANES), smt_ref[...], alog_row[...], dtb_row[...],
                    alog_col[...], dtb_col[...], ls, bt * ls)
    u_base, k_cum, qk, q_dec, k_dec, gl = _gdn_tiles(conv, gt)
    for h in range(GDN_HEADS):
        us, ois = [], []
        for bi in range(bt):
            rows = slice(bi * ls, (bi + 1) * ls)
            state = s0_ref[0, bi, h]
            u = u_base[h][rows] - _dot(k_cum[h][rows], state)
            ois.append(_dot(q_dec[h][rows], state))
            s_out_ref[0, bi, h] = state * jnp.exp(gl[h][bi * ls:bi * ls + 1, :]) + _dot_tn(k_dec[h][rows], u)
            us.append(u)
        o = jnp.concatenate(ois, axis=0) + _dot(_bf(qk[h]), _bf(jnp.concatenate(us, axis=0)))
        hd = GDN_HEAD_DIM
        cols = slice(h * hd, (h + 1) * hd)
        gate = _silu(gg_ref[:, :, cols].reshape(bt * ls, hd))
        go_ref[:, :, cols] = (_rms(o, nw_ref[...]) * gate).reshape(bt, ls, hd)


def _gdn_sample(layer, qkv, hist, sm, smt, gg, s0_all, cw, alog_row, dtb_row, alog_col, dtb_col, nw, prev_states, bt):
    bs, ls, _ = qkv.shape
    per = lambda n: pl.BlockSpec((bt, ls, n), lambda i: (i, 0, 0))
    full = lambda a: pl.BlockSpec(a.shape, lambda i: (0,) * a.ndim)
    hd = GDN_HEAD_DIM
    st = pl.BlockSpec((1, bt, GDN_HEADS, hd, hd), lambda i: (layer, i, 0, 0, 0))
    args = [qkv, hist, sm, smt, gg, s0_all, cw, alog_row, dtb_row, alog_col, dtb_col, nw]
    in_specs = [per(GDN_CONV_DIM), pl.BlockSpec((bt, SUBLANES, GDN_CONV_DIM), lambda i: (i, 0, 0)), per(LANES),
                pl.BlockSpec((SMALL_COLS, bt * ls), lambda i: (0, i)), per(GDN_WIDTH), st,
                full(cw), full(alog_row), full(dtb_row), full(alog_col), full(dtb_col), full(nw)]
    aliases = {}
    if prev_states is not None:
        aliases = {len(args): 1}
        args.append(prev_states)
        in_specs.append(pl.BlockSpec(memory_space=pl.ANY))
    return pl.pallas_call(
        functools.partial(_gdn_sample_kernel, ls=ls), grid=(bs // bt,), in_specs=in_specs,
        out_specs=(per(GDN_WIDTH), st),
        out_shape=(jax.ShapeDtypeStruct((bs, ls, GDN_WIDTH), F32), jax.ShapeDtypeStruct(s0_all.shape, F32)),
        scratch_shapes=[pltpu.VMEM((bt, SUBLANES + ls, GDN_CONV_DIM), F32)],
        input_output_aliases=aliases, compiler_params=_params("arbitrary"), name="gdn_sample")(*args)


def _norm_matmul_kernel(x_ref, g_ref, w_ref, *o_refs):
    xn = _rms(x_ref[...], g_ref[...]).astype(BF16)
    n = w_ref.shape[1] // len(o_refs)
    for i, o_ref in enumerate(o_refs):
        o_ref[...] = _dot(xn, w_ref[:, i * n:(i + 1) * n])


def _norm_matmul(x, g, w, n_out, tm):
    m = x.shape[0]
    n = w.shape[1] // n_out
    return pl.pallas_call(
        _norm_matmul_kernel, grid=(m // tm,),
        in_specs=[pl.BlockSpec((tm, x.shape[1]), lambda i: (i, 0)), pl.BlockSpec(g.shape, lambda i: (0, 0)),
                  pl.BlockSpec(w.shape, lambda i: (0, 0))],
        out_specs=tuple(pl.BlockSpec((tm, n), lambda i: (i, 0)) for _ in range(n_out)),
        out_shape=tuple(jax.ShapeDtypeStruct((m, n), F32) for _ in range(n_out)),
        compiler_params=_params("arbitrary"), name="norm_matmul")(x, g, w)


def _matmul_residual_kernel(*refs, n_in):
    x_ref, o_ref = refs[0], refs[-1]
    acc = x_ref[...]
    for a_ref, w_ref in zip(refs[1:1 + n_in], refs[1 + n_in:1 + 2 * n_in]):
        acc = acc + _dot(a_ref[...].astype(BF16), w_ref[...])
    o_ref[...] = acc


def _matmul_residual(x, a_list, w_list, tm):
    m, d = x.shape
    n_in = len(a_list)
    in_specs = [pl.BlockSpec((tm, d), lambda i: (i, 0))]
    in_specs += [pl.BlockSpec((tm, a.shape[1]), lambda i: (i, 0)) for a in a_list]
    in_specs += [pl.BlockSpec(w.shape, lambda i: (0, 0)) for w in w_list]
    return pl.pallas_call(
        functools.partial(_matmul_residual_kernel, n_in=n_in), grid=(m // tm,), in_specs=in_specs,
        out_specs=pl.BlockSpec((tm, d), lambda i: (i, 0)), out_shape=jax.ShapeDtypeStruct((m, d), F32),
        compiler_params=_params("arbitrary"), name="matmul_residual")(x, *a_list, *w_list)


def _mem_attn_prompt_kernel(x_ref, g_ref, wq_ref, wo_ref, mk_ref, mv_ref, o_ref):
    x = x_ref[0]
    xn = _rms(x, g_ref[...]).astype(BF16)
    q = _dot(xn, wq_ref[...]).astype(BF16)
    outs = []
    for h in range(MEM_HEADS):
        cols = slice(h * MEM_HEAD_DIM, (h + 1) * MEM_HEAD_DIM)
        s = _dot_nt(q[:, cols], mk_ref[0, :, cols].astype(BF16)) * (MEM_HEAD_DIM ** -0.5)
        p = jnp.exp(s - jnp.max(s, axis=1, keepdims=True))
        o = _dot(p.astype(BF16), mv_ref[0, :, cols].astype(BF16))
        outs.append(o / jnp.sum(p, axis=1, keepdims=True))
    o_ref[0] = x + _dot(jnp.concatenate(outs, axis=1).astype(BF16), wo_ref[...])


def _mem_attn_prompt(x, g, wq, wo, mk, mv, tm):
    b, l, d = x.shape
    n_mem = mk.shape[1]
    full = lambda a: pl.BlockSpec(a.shape, lambda bi, i: (0,) * a.ndim)
    mem_spec = pl.BlockSpec((1, n_mem, MEM_WIDTH), lambda bi, i: (bi, 0, 0))
    x_spec = pl.BlockSpec((1, tm, d), lambda bi, i: (bi, i, 0))
    return pl.pallas_call(
        _mem_attn_prompt_kernel, grid=(b, l // tm),
        in_specs=[x_spec, full(g), full(wq), full(wo), mem_spec, mem_spec],
        out_specs=x_spec, out_shape=jax.ShapeDtypeStruct(x.shape, F32),
        compiler_params=_params("arbitrary", "arbitrary"), name="mem_attn_prompt")(x, g, wq, wo, mk, mv)


def _mem_attn_sample_kernel(q_ref, mk_ref, mv_ref, o_ref):
    bt, ls, _ = q_ref.shape
    rows = MEM_HEADS * ls
    n = mk_ref.shape[2]
    hd = MEM_HEAD_DIM
    own_head = _mod(_iota((rows, n), 1), MEM_HEADS) == _div(_iota((rows, n), 0), ls)
    for bi in range(bt):
        q = q_ref[bi]
        qs = jnp.concatenate([q[:, h * hd:(h + 1) * hd] for h in range(MEM_HEADS)], axis=0)
        s = _dot_nt(qs.astype(BF16), mk_ref[0, bi].astype(BF16)) * (hd ** -0.5)
        s = jnp.where(own_head, s, -jnp.inf)
        p = jnp.exp(s - jnp.max(s, axis=1, keepdims=True))
        o = _dot(p.astype(BF16), mv_ref[0, bi].astype(BF16)) / jnp.sum(p, axis=1, keepdims=True)
        o_ref[bi] = jnp.concatenate([o[h * ls:(h + 1) * ls] for h in range(MEM_HEADS)], axis=1)


def _mem_attn_sample(layer, q, mk_all, mv_all, bt):
    bs, ls, _ = q.shape
    mem_spec = pl.BlockSpec((1, bt) + mk_all.shape[2:], lambda i: (layer, i, 0, 0))
    q_spec = pl.BlockSpec((bt, ls, MEM_WIDTH), lambda i: (i, 0, 0))
    return pl.pallas_call(
        _mem_attn_sample_kernel, grid=(bs // bt,), in_specs=[q_spec, mem_spec, mem_spec],
        out_specs=q_spec, out_shape=jax.ShapeDtypeStruct(q.shape, F32),
        compiler_params=_params("arbitrary"), name="mem_attn_sample")(q, mk_all, mv_all)


def _ffn_kernel(x_ref, g_ref, wa_ref, wu_ref, wo_ref, gf_ref, o_ref, *, chunk, final_norm):
    x = x_ref[...]
    xn = _rms(x, g_ref[...]).astype(BF16)
    n = wa_ref.shape[1] // chunk
    acc = x
    act = None
    for c in range(n + 1):
        if c < n:
            cols = slice(c * chunk, (c + 1) * chunk)
            a = _dot(xn, wa_ref[:, cols])
            u = _dot(xn, wu_ref[:, cols])
        if c > 0:
            acc = acc + _dot(act, wo_ref[(c - 1) * chunk:c * chunk, :])
        if c < n:
            act = (_silu(a) * u).astype(BF16)
    if final_norm:
        acc = _rms(acc, gf_ref[...])
    o_ref[...] = acc


def _ffn(x, g, wa, wu, wo, gf, final_norm, tm):
    m, d = x.shape
    full = lambda a: pl.BlockSpec(a.shape, lambda i: (0,) * a.ndim)
    row = pl.BlockSpec((tm, d), lambda i: (i, 0))
    return pl.pallas_call(
        functools.partial(_ffn_kernel, chunk=2 * LANES, final_norm=final_norm), grid=(m // tm,),
        in_specs=[row, full(g), full(wa), full(wu), full(wo), full(gf)], out_specs=row,
        out_shape=jax.ShapeDtypeStruct((m, d), F32), compiler_params=_params("arbitrary"), name="ffn")(
            x, g, wa, wu, wo, gf)


def _pick(m, cap):
    t = min(m, cap)
    while m % t:
        t -= SUBLANES
    assert t > 0 and m % t == 0, (m, cap)
    return t


def _pad_lanes(v, offset):
    row = jnp.zeros((1, LANES), F32).at[0, offset:offset + v.shape[0]].set(v)
    return row, row[0, :SMALL_COLS].reshape(SMALL_COLS, 1)


def kernel(x_prompt, x_sample, cache_fox_k, cache_fox_v, cache_fox_logf, state_gdn, state_gdn_conv, cache_mem_k, cache_mem_v, page_table, mem_prompt, g_norm_mix, w_in, b_fox_f, gdn_conv_w, gdn_a_log, gdn_dt_bias, gdn_norm_w, w_out, g_norm_memin, w_mem_kv, g_norm_mem, w_mem_q, w_mem_o, g_norm_ffn, w_ffn_in, w_ffn_out, g_final):
    bp, lp, d = x_prompt.shape
    bs, ls, _ = x_sample.shape
    depth = w_in.shape[0]
    n_phys, page = cache_fox_k.shape[1], cache_fox_k.shape[2]
    n_pages = page_table.shape[1]
    n_mem = mem_prompt.shape[1]
    d_ff = w_ffn_out.shape[1]
    mp, ms = bp * lp, bs * ls
    hd = GDN_HEAD_DIM

    yp = x_prompt.reshape(mp, d)
    ys = x_sample.reshape(ms, d)
    mem_flat = mem_prompt.reshape(bp * n_mem, d)
    cache_k = cache_fox_k.transpose(0, 1, 3, 4, 2).reshape(depth, n_phys, FOX_WIDTH, page)
    cache_v = cache_fox_v.transpose(0, 1, 3, 4, 2).reshape(depth, n_phys, FOX_WIDTH, page)
    page_rows = depth * n_phys * FOX_HEADS
    cache_cum = _cumsum_pages(cache_fox_logf.transpose(0, 1, 3, 2).reshape(page_rows, page), _pick(page_rows, 512))
    cache_cum = cache_cum.reshape(depth, n_phys, FOX_HEADS, page)
    tokens_per_row = LANES // FOX_HEADS
    hist = jnp.pad(state_gdn_conv, ((0, 0), (0, 0), (SUBLANES - (CONV_WIDTH - 1), 0), (0, 0)))
    mem_k_cache = cache_mem_k.reshape(depth, bs, n_mem * MEM_HEADS, MEM_HEAD_DIM)
    mem_v_cache = cache_mem_v.reshape(depth, bs, n_mem * MEM_HEADS, MEM_HEAD_DIM)

    o1 = FOX_WIDTH * 3
    o2 = o1 + FOX_HEADS
    o3 = o2 + GDN_CONV_DIM
    o4 = o3 + 2 * GDN_HEADS
    row2 = lambda v: v.reshape(1, -1)

    outs_p = {k: [] for k in ("lf", "gs", "gc", "mk", "mv")}
    outs_s = {k: [] for k in ("fk", "fv", "lf", "gc")}
    kt_all = vt_all = states_s = None
    for l in range(depth):
        w = w_in[l]
        wq = w[:, :FOX_WIDTH].astype(BF16)
        wqt = w[:, :FOX_WIDTH].T.astype(BF16)
        wkv = w[:, FOX_WIDTH:o1].astype(BF16)
        wk = w[:, FOX_WIDTH:2 * FOX_WIDTH].astype(BF16)
        wkvt = w[:, FOX_WIDTH:o1].T.astype(BF16)
        wg = jnp.concatenate([w[:, o2:o3], w[:, o4:]], axis=1).astype(BF16)
        w_small = jnp.concatenate([w[:, o1:o2], w[:, o3:o4]], axis=1)
        ws = jnp.pad(w_small, ((0, 0), (0, LANES - SMALL_COLS))).astype(BF16)
        wst = w_small.T.astype(BF16)
        b_row = row2(b_fox_f[l])
        b_col = b_fox_f[l].reshape(FOX_HEADS, 1)
        b_pad = jnp.pad(b_row, ((0, 0), (0, LANES - FOX_HEADS)))
        alog_row, alog_col = _pad_lanes(gdn_a_log[l], GA_COL)
        dtb_row, dtb_col = _pad_lanes(gdn_dt_bias[l], GA_COL)
        cw = gdn_conv_w[l]
        nw = row2(gdn_norm_w[l])
        wo_fox = w_out[l][:FOX_WIDTH].astype(BF16)
        wo_gdn = w_out[l][FOX_WIDTH:].astype(BF16)
        w_kv = w_mem_kv[l].astype(BF16)
        w_q = w_mem_q[l].astype(BF16)
        w_o = w_mem_o[l].astype(BF16)
        wa = w_ffn_in[l][:, :d_ff].astype(BF16)
        wu = w_ffn_in[l][:, d_ff:].astype(BF16)
        wf = w_ffn_out[l].astype(BF16)
        g_mix, g_memin, g_mem, g_ffn = row2(g_norm_mix[l]), row2(g_norm_memin[l]), row2(g_norm_mem[l]), row2(g_norm_ffn[l])
        gf = row2(g_final)
        last = l == depth - 1

        kt_all, vt_all, kaug, qtaug, vtaug, qkv, gg, lft, sm, smt = _in_proj_prompt(
            l, depth, bp, yp, g_mix, wqt, wk, wkvt, wg, ws, wst, b_row, b_col, kt_all, vt_all, _pick(lp, 256))
        fo = _fox_prompt(kaug, qtaug, vtaug, _pick(lp, 512)).reshape(mp, FOX_WIDTH)
        go, s_new = _gdn_prompt(qkv, sm, smt, gg, cw, alog_row, dtb_row, alog_col, dtb_col, nw, bp, _pick(lp, 256))
        yp = _matmul_residual(yp, [fo, go], [wo_fox, wo_gdn], _pick(mp, 512))
        mk, mv = _norm_matmul(mem_flat, g_memin, w_kv, 2, _pick(bp * n_mem, 256))
        yp = _mem_attn_prompt(yp.reshape(bp, lp, d), g_mem, w_q, w_o, mk.reshape(bp, n_mem, MEM_WIDTH),
                              mv.reshape(bp, n_mem, MEM_WIDTH), _pick(lp, 512)).reshape(mp, d)
        yp = _ffn(yp, g_ffn, wa, wu, wf, gf, last, _pick(mp, 256))
        outs_p["lf"].append(lft.transpose(0, 2, 1))
        outs_p["gs"].append(s_new)
        outs_p["gc"].append(qkv.reshape(bp, lp, GDN_CONV_DIM)[:, lp - (CONV_WIDTH - 1):])
        outs_p["mk"].append(mk.reshape(bp, n_mem, MEM_HEADS, MEM_HEAD_DIM))
        outs_p["mv"].append(mv.reshape(bp, n_mem, MEM_HEADS, MEM_HEAD_DIM))

        fqb, fk, fv, qkv, gg, lf, sm, smt = _in_proj_sample(ys, g_mix, wq, wkv, wg, ws, wst, b_row, _pick(ms, 256))
        s3 = lambda a: a.reshape(bs, ls, a.shape[-1])
        fo = _fox_decode(l, page_table, s3(fqb), s3(fk), s3(fv), s3(sm), b_pad, cache_k, cache_v,
                         cache_cum).reshape(ms, FOX_WIDTH)
        go, states_s = _gdn_sample(l, s3(qkv), hist[l], s3(sm), smt, s3(gg), state_gdn, cw, alog_row, dtb_row,
                                   alog_col, dtb_col, nw, states_s, _pick(bs, 16))
        ys = _matmul_residual(ys, [fo, go.reshape(ms, GDN_WIDTH)], [wo_fox, wo_gdn], _pick(ms, 512))
        (q_mem,) = _norm_matmul(ys, g_mem, w_q, 1, _pick(ms, 256))
        o_mem = _mem_attn_sample(l, s3(q_mem), mem_k_cache, mem_v_cache, _pick(bs, 8)).reshape(ms, MEM_WIDTH)
        ys = _matmul_residual(ys, [o_mem], [w_o], _pick(ms, 512))
        ys = _ffn(ys, g_ffn, wa, wu, wf, gf, last, _pick(ms, 256))
        outs_s["fk"].append(fk.reshape(bs, ls, FOX_HEADS, FOX_HEAD_DIM))
        outs_s["fv"].append(fv.reshape(bs, ls, FOX_HEADS, FOX_HEAD_DIM))
        outs_s["lf"].append(lf.reshape(bs, ls, FOX_HEADS))
        outs_s["gc"].append(s3(qkv)[:, ls - (CONV_WIDTH - 1):])

    st = jnp.stack
    heads_last = lambda a: a.reshape(depth, bp, FOX_HEADS, FOX_HEAD_DIM, lp).transpose(0, 1, 4, 2, 3)
    return (yp.reshape(bp, lp, d), ys.reshape(bs, ls, d),
            heads_last(kt_all), heads_last(vt_all), st(outs_p["lf"]), st(outs_p["gs"]), st(outs_p["gc"]),
            st(outs_p["mk"]), st(outs_p["mv"]),
            st(outs_s["fk"]), st(outs_s["fv"]), st(outs_s["lf"]), states_s, st(outs_s["gc"]))
```

```python
import functools
import math

import jax
import jax.numpy as jnp
from jax import lax
from jax.experimental import pallas as pl
from jax.experimental.pallas import tpu as pltpu

F32 = jnp.float32
BF16 = jnp.bfloat16
HI = lax.Precision.HIGHEST

EPS = 1e-6
D_MODEL = 1024
FOX_HEADS = 8
FOX_HEAD_DIM = 64
FOX_WIDTH = FOX_HEADS * FOX_HEAD_DIM
GDN_HEADS = 4
GDN_HEAD_DIM = 128
GDN_WIDTH = GDN_HEADS * GDN_HEAD_DIM
GDN_CONV_DIM = 3 * GDN_WIDTH
CONV_WIDTH = 4
GDN_CHUNK = 64
MEM_HEADS = 4
MEM_HEAD_DIM = 128
MEM_WIDTH = MEM_HEADS * MEM_HEAD_DIM
LANES = 128
SUBLANES = 8
SMALL_COLS = 16
GA_COL = FOX_HEADS
GB_COL = FOX_HEADS + GDN_HEADS
VMEM_LIMIT = 56 * 1024 * 1024


def _params(*sem):
    return pltpu.CompilerParams(dimension_semantics=sem, vmem_limit_bytes=VMEM_LIMIT)


def _dot(a, b):
    return jnp.dot(a, b, preferred_element_type=F32)


def _dot_nt(a, b):
    return lax.dot_general(a, b, (((1,), (1,)), ((), ())), preferred_element_type=F32)


def _dot_tn(a, b):
    return lax.dot_general(a, b, (((0,), (0,)), ((), ())), preferred_element_type=F32)


def _dot_hi(a, b):
    return jnp.dot(a, b, precision=HI, preferred_element_type=F32)


def _dot_nt_hi(a, b):
    return lax.dot_general(a, b, (((1,), (1,)), ((), ())), precision=HI, preferred_element_type=F32)


def _rms(x, g):
    return x * lax.rsqrt(jnp.mean(x * x, axis=-1, keepdims=True) + EPS) * g


def _log_sigmoid(x):
    return jnp.minimum(x, 0.0) - jnp.log1p(jnp.exp(-jnp.abs(x)))


def _softplus(x):
    return jnp.maximum(x, 0.0) + jnp.log1p(jnp.exp(-jnp.abs(x)))


def _silu(x):
    return x * jax.nn.sigmoid(x)


def _iota(shape, dim):
    return lax.broadcasted_iota(jnp.int32, shape, dim)


def _log2(n):
    lg = n.bit_length() - 1
    assert n == 1 << lg, n
    return lg


def _div(x, n):
    return x >> _log2(n)


def _mod(x, n):
    _log2(n)
    return x & (n - 1)


def _in_proj_common(x_ref, g_ref, wq_ref, wg_ref, ws_ref, wst_ref, b_ref,
                    fqb_ref, qkv_ref, gg_ref, lf_ref, sm_ref, smt_ref):
    xn = _rms(x_ref[...], g_ref[...]).astype(BF16)
    fqb_ref[...] = (_dot(xn, wq_ref[...]) * (FOX_HEAD_DIM ** -0.5)).astype(BF16)
    qkv_ref[...] = _dot(xn, wg_ref[:, 0:GDN_CONV_DIM])
    gg_ref[...] = _dot(xn, wg_ref[:, GDN_CONV_DIM:])
    sm = _dot(xn, ws_ref[...])
    smt = _dot_nt(wst_ref[...], xn)
    sm_ref[...] = sm
    smt_ref[...] = smt
    lf_ref[...] = _log_sigmoid(sm[:, 0:FOX_HEADS] + b_ref[...])
    return xn, smt


LOG2E = 1.4426950408889634
AUG_ROWS = SUBLANES
VALUE_ROWS = FOX_HEAD_DIM + 2 * SUBLANES


def _in_proj_prompt_kernel(x_ref, g_ref, wqt_ref, wk_ref, wkvt_ref, wg_ref, ws_ref, wst_ref, b_ref, bcol_ref,
                           *rest, tiles_per_seq):
    (kt_ref, vt_ref, kaug_ref, qtaug_ref, vtaug_ref, qkv_ref, gg_ref, lft_ref, sm_ref, smt_ref,
     crow_ref, ccol_ref) = rest[-12:]
    tm = x_ref.shape[0]
    hd = FOX_HEAD_DIM

    @pl.when(pl.program_id(0) % tiles_per_seq == 0)
    def _():
        crow_ref[...] = jnp.zeros_like(crow_ref)
        ccol_ref[...] = jnp.zeros_like(ccol_ref)

    xn = _rms(x_ref[...], g_ref[...]).astype(BF16)
    sm = _dot(xn, ws_ref[...])
    smt = _dot_nt(wst_ref[...], xn)
    sm_ref[...] = sm
    smt_ref[...] = smt
    lf_col = _log_sigmoid(sm[:, 0:FOX_HEADS] + b_ref[...])
    lf_row = _log_sigmoid(smt[0:FOX_HEADS, :] + bcol_ref[...])
    lft_ref[0] = lf_row
    kt = _dot_nt(wkvt_ref[0:FOX_WIDTH, :], xn)
    kt_ref[0, 0] = kt
    vt = _dot_nt(wkvt_ref[FOX_WIDTH:, :], xn)
    vt_ref[0, 0] = vt
    k_tok = _dot(xn, wk_ref[...])
    qt = _dot_nt(wqt_ref[...], xn) * (hd ** -0.5 * LOG2E)

    r = _iota((tm, tm), 0)
    c = _iota((tm, tm), 1)
    c_col = sum(_dot(_bf(c <= r), part) for part in _split3(lf_col)) + crow_ref[...]
    c_row = sum(_dot(part, _bf(r <= c)) for part in _split3(lf_row)) + ccol_ref[...]
    crow_ref[...] = c_col[tm - 1:tm, :]
    ccol_ref[...] = c_row[:, tm - 1:tm]

    lane = _iota((tm, LANES), 1)
    sub = _iota((AUG_ROWS, tm), 0)
    ones_key = jnp.where((lane >= hd + 3) & (lane < hd + 6), 1.0, 0.0)
    zeros_tail = jnp.zeros((LANES - hd - AUG_ROWS, tm), F32)
    value_tail = jnp.where(_iota((VALUE_ROWS - hd, tm), 0) == 0, 1.0, 0.0)
    gdn_cols = wg_ref.shape[1] // FOX_HEADS
    for h in range(FOX_HEADS):
        cols = slice(h * gdn_cols, (h + 1) * gdn_cols)
        part = _dot(xn, wg_ref[:, cols])
        if (h + 1) * gdn_cols <= GDN_CONV_DIM:
            qkv_ref[:, cols] = part
        else:
            gg_ref[:, h * gdn_cols - GDN_CONV_DIM:(h + 1) * gdn_cols - GDN_CONV_DIM] = part
        pair = k_tok[:, (h // 2) * LANES:(h // 2 + 1) * LANES]
        k_h = pair if h % 2 == 0 else pltpu.roll(pair, hd, 1)
        k1, k2, k3 = [p.astype(F32) for p in _split3(c_col[:, h:h + 1] * (-LOG2E))]
        extra = jnp.where(lane == hd, k1, jnp.where(lane == hd + 1, k2, jnp.where(lane == hd + 2, k3, ones_key)))
        kaug_ref[0, h] = jnp.where(lane < hd, k_h, extra).astype(BF16)
        q1, q2, q3 = [p.astype(F32) for p in _split3(c_row[h:h + 1, :] * LOG2E)]
        query_tail = jnp.where(sub < 3, 1.0, jnp.where(sub == 3, q1, jnp.where(sub == 4, q2,
                                                                           jnp.where(sub == 5, q3, 0.0))))
        rows = slice(h * hd, (h + 1) * hd)
        qtaug_ref[0, h] = jnp.concatenate([qt[rows], query_tail, zeros_tail], axis=0).astype(BF16)
        vtaug_ref[0, h] = jnp.concatenate([vt[rows], value_tail], axis=0).astype(BF16)


def _in_proj_prompt(layer, depth, batch, x, g, wqt, wk, wkvt, wg, ws, wst, b_row, b_col, prev_kt, prev_vt, tm):
    m = x.shape[0]
    l = m // batch
    nt = l // tm
    row = lambda n: pl.BlockSpec((tm, n), lambda i: (i, 0))
    full = lambda a: pl.BlockSpec(a.shape, lambda i: (0,) * a.ndim)
    kv_all = pl.BlockSpec((1, 1, FOX_WIDTH, tm), lambda i: (layer, i // nt, 0, i % nt))
    feat_major = pl.BlockSpec((1, FOX_HEADS, LANES, tm), lambda i: (i // nt, 0, 0, i % nt))
    out_shape = (
        jax.ShapeDtypeStruct((depth, batch, FOX_WIDTH, l), F32),
        jax.ShapeDtypeStruct((depth, batch, FOX_WIDTH, l), F32),
        jax.ShapeDtypeStruct((batch, FOX_HEADS, l, LANES), BF16),
        jax.ShapeDtypeStruct((batch, FOX_HEADS, LANES, l), BF16),
        jax.ShapeDtypeStruct((batch, FOX_HEADS, VALUE_ROWS, l), BF16),
        jax.ShapeDtypeStruct((m, GDN_CONV_DIM), F32),
        jax.ShapeDtypeStruct((m, GDN_WIDTH), F32),
        jax.ShapeDtypeStruct((batch, FOX_HEADS, l), F32),
        jax.ShapeDtypeStruct((m, LANES), F32),
        jax.ShapeDtypeStruct((SMALL_COLS, m), F32),
    )
    out_specs = (kv_all, kv_all,
                 pl.BlockSpec((1, FOX_HEADS, tm, LANES), lambda i: (i // nt, 0, i % nt, 0)), feat_major,
                 pl.BlockSpec((1, FOX_HEADS, VALUE_ROWS, tm), lambda i: (i // nt, 0, 0, i % nt)),
                 row(GDN_CONV_DIM), row(GDN_WIDTH),
                 pl.BlockSpec((1, FOX_HEADS, tm), lambda i: (i // nt, 0, i % nt)), row(LANES),
                 pl.BlockSpec((SMALL_COLS, tm), lambda i: (0, i)))
    args = [x, g, wqt, wk, wkvt, wg, ws, wst, b_row, b_col]
    in_specs = [row(D_MODEL)] + [full(a) for a in args[1:]]
    aliases = {}
    if prev_kt is not None:
        aliases = {len(args): 0, len(args) + 1: 1}
        args += [prev_kt, prev_vt]
        in_specs += [pl.BlockSpec(memory_space=pl.ANY)] * 2
    return pl.pallas_call(
        functools.partial(_in_proj_prompt_kernel, tiles_per_seq=nt), grid=(m // tm,), in_specs=in_specs,
        out_specs=out_specs, out_shape=out_shape, input_output_aliases=aliases,
        scratch_shapes=[pltpu.VMEM((1, FOX_HEADS), F32), pltpu.VMEM((FOX_HEADS, 1), F32)],
        compiler_params=_params("arbitrary"), name="in_proj_prompt")(*args)


def _in_proj_sample_kernel(x_ref, g_ref, wq_ref, wkv_ref, wg_ref, ws_ref, wst_ref, b_ref,
                           fqb_ref, fk_ref, fv_ref, qkv_ref, gg_ref, lf_ref, sm_ref, smt_ref):
    xn, _ = _in_proj_common(x_ref, g_ref, wq_ref, wg_ref, ws_ref, wst_ref, b_ref,
                            fqb_ref, qkv_ref, gg_ref, lf_ref, sm_ref, smt_ref)
    fk_ref[...] = _dot(xn, wkv_ref[:, 0:FOX_WIDTH])
    fv_ref[...] = _dot(xn, wkv_ref[:, FOX_WIDTH:])


def _in_proj_sample(x, g, wq, wkv, wg, ws, wst, b_row, tm):
    m = x.shape[0]
    row = lambda n: pl.BlockSpec((tm, n), lambda i: (i, 0))
    full = lambda a: pl.BlockSpec(a.shape, lambda i: (0,) * a.ndim)
    out_shape = (
        jax.ShapeDtypeStruct((m, FOX_WIDTH), BF16),
        jax.ShapeDtypeStruct((m, FOX_WIDTH), F32),
        jax.ShapeDtypeStruct((m, FOX_WIDTH), F32),
        jax.ShapeDtypeStruct((m, GDN_CONV_DIM), F32),
        jax.ShapeDtypeStruct((m, GDN_WIDTH), F32),
        jax.ShapeDtypeStruct((m, FOX_HEADS), F32),
        jax.ShapeDtypeStruct((m, LANES), F32),
        jax.ShapeDtypeStruct((SMALL_COLS, m), F32),
    )
    out_specs = (row(FOX_WIDTH), row(FOX_WIDTH), row(FOX_WIDTH), row(GDN_CONV_DIM), row(GDN_WIDTH),
                 row(FOX_HEADS), row(LANES), pl.BlockSpec((SMALL_COLS, tm), lambda i: (0, i)))
    args = [x, g, wq, wkv, wg, ws, wst, b_row]
    return pl.pallas_call(
        _in_proj_sample_kernel, grid=(m // tm,), in_specs=[row(D_MODEL)] + [full(a) for a in args[1:]],
        out_specs=out_specs, out_shape=out_shape, compiler_params=_params("arbitrary"),
        name="in_proj_sample")(*args)


def _cumsum_pages_kernel(x_ref, o_ref):
    n = x_ref.shape[1]
    upto = _bf(_iota((n, n), 0) <= _iota((n, n), 1))
    o_ref[...] = sum(_dot(part, upto) for part in _split3(x_ref[...]))


def _cumsum_pages(x, block_rows):
    rows, n = x.shape
    spec = pl.BlockSpec((block_rows, n), lambda i: (i, 0))
    return pl.pallas_call(
        _cumsum_pages_kernel, grid=(rows // block_rows,), in_specs=[spec], out_specs=spec,
        out_shape=jax.ShapeDtypeStruct(x.shape, F32), compiler_params=_params("arbitrary"),
        name="cumsum_pages")(x)


FOX_UNIT_Q = 256
FOX_UNIT_GROUP = 8


def _fox_prompt_kernel(qi_ref, ki_ref, k_ref, qt_ref, vt_ref, o_ref, m_ref, acc_ref, *, tq, tk):
    step_id = pl.program_id(1)
    qi = qi_ref[step_id]
    ki = ki_ref[step_id]
    uq = min(FOX_UNIT_Q, tq)
    hd = FOX_HEAD_DIM

    @pl.when(ki == 0)
    def _():
        m_ref[...] = jnp.full_like(m_ref, -jnp.inf)
        acc_ref[...] = jnp.zeros_like(acc_ref)

    units = [(h, qc) for h in range(FOX_HEADS) for qc in range(tq // uq)]

    def step(diagonal):
        for g0 in range(0, len(units), FOX_UNIT_GROUP):
            group = units[g0:g0 + FOX_UNIT_GROUP]
            cols = [slice(qc * uq, (qc + 1) * uq) for _, qc in group]
            keys = [slice(0, (qc + 1) * uq if diagonal else tk) for _, qc in group]
            ts = [_dot(k_ref[0, h, ks], qt_ref[0, h, :, cs_]) for (h, _), cs_, ks in zip(group, cols, keys)]
            if diagonal:
                ts = [jnp.where(_iota(t.shape, 0) <= _iota(t.shape, 1) + qc * uq, t, -jnp.inf)
                      for t, (_, qc) in zip(ts, group)]
            m_old = [m_ref[h, :, cs_] for (h, _), cs_ in zip(group, cols)]
            m_new = [jnp.maximum(mo, jnp.max(t, axis=0, keepdims=True)) for mo, t in zip(m_old, ts)]
            ps = [jnp.exp2(t - mn).astype(BF16) for t, mn in zip(ts, m_new)]
            pvs = [_dot(vt_ref[0, h, :, ks], p) for (h, _), ks, p in zip(group, keys, ps)]
            for (h, _), cs_, mo, mn, pv in zip(group, cols, m_old, m_new, pvs):
                acc_ref[h, :, cs_] = jnp.exp2(mo - mn) * acc_ref[h, :, cs_] + pv
                m_ref[h, :, cs_] = mn

    @pl.when(ki < qi)
    def _():
        step(False)

    @pl.when(ki == qi)
    def _():
        step(True)
        for j in range(FOX_HEADS // 2):
            pair = [acc_ref[h, 0:hd, :] / acc_ref[h, hd:hd + 1, :] for h in (2 * j, 2 * j + 1)]
            o_ref[0, :, j * LANES:(j + 1) * LANES] = jnp.concatenate(pair, axis=0).T


def _fox_prompt(kaug, qtaug, vtaug, tq):
    b, _, l, _ = kaug.shape
    tk = tq
    n = l // tq
    pairs = [(qi, ki) for qi in range(n) for ki in range(qi + 1)]
    qi_tab = jnp.asarray([p[0] for p in pairs], jnp.int32)
    ki_tab = jnp.asarray([p[1] for p in pairs], jnp.int32)
    grid_spec = pltpu.PrefetchScalarGridSpec(
        num_scalar_prefetch=2, grid=(b, len(pairs)),
        in_specs=[pl.BlockSpec((1, FOX_HEADS, tk, LANES), lambda bi, s, qt, kt: (bi, 0, kt[s], 0)),
                  pl.BlockSpec((1, FOX_HEADS, LANES, tq), lambda bi, s, qt, kt: (bi, 0, 0, qt[s])),
                  pl.BlockSpec((1, FOX_HEADS, VALUE_ROWS, tk), lambda bi, s, qt, kt: (bi, 0, 0, kt[s]))],
        out_specs=pl.BlockSpec((1, tq, FOX_WIDTH), lambda bi, s, qt, kt: (bi, qt[s], 0)),
        scratch_shapes=[pltpu.VMEM((FOX_HEADS, 1, tq), F32), pltpu.VMEM((FOX_HEADS, VALUE_ROWS, tq), F32)])
    return pl.pallas_call(
        functools.partial(_fox_prompt_kernel, tq=tq, tk=tk), grid_spec=grid_spec,
        out_shape=jax.ShapeDtypeStruct((b, l, FOX_WIDTH), F32),
        compiler_params=_params("arbitrary", "arbitrary"), name="fox_prompt")(qi_tab, ki_tab, kaug, qtaug, vtaug)


def _fox_decode_kernel(pt_ref, q_ref, kn_ref, vn_ref, sm_ref, b_ref, *rest, n_pages, page, ls):
    k_refs, v_refs = rest[0:n_pages], rest[n_pages:2 * n_pages]
    cum_ref, o_ref = rest[2 * n_pages:]
    sample = pl.program_id(0)
    rows = FOX_HEADS * ls
    headmask = _div(_iota((rows, FOX_WIDTH), 0), ls) == _div(_iota((rows, FOX_WIDTH), 1), FOX_HEAD_DIM)
    expand = (_div(_iota((rows, FOX_HEADS), 0), ls) == _iota((rows, FOX_HEADS), 1)).astype(F32)
    q = q_ref[0].astype(F32)
    qbd = jnp.where(headmask, jnp.concatenate([q] * FOX_HEADS, axis=0), 0.0).astype(BF16)

    lane = _iota((ls, LANES), 1)
    lf_new = jnp.where(lane < FOX_HEADS, _log_sigmoid(sm_ref[0] + b_ref[...]), 0.0)
    tok = _iota((ls, LANES), 0)
    cum_new = lf_new
    for s in range(1, ls):
        cum_new = cum_new + jnp.where(tok >= s, pltpu.roll(lf_new, s, 0), 0.0)
    cum_new = cum_new[:, 0:FOX_HEADS]
    query_of_row = (_mod(_iota((rows, ls), 0), ls) == _iota((rows, ls), 1)).astype(F32)
    cq_row = jnp.sum(_dot_hi(query_of_row, cum_new) * expand, axis=1, keepdims=True)

    vis = _iota((rows, ls), 1) <= _mod(_iota((rows, ls), 0), ls)
    tiles = [jnp.where(vis, _dot_nt(qbd, kn_ref[0].astype(BF16)) - _dot_nt_hi(expand, cum_new), -jnp.inf)]
    reach = jnp.zeros((FOX_HEADS, 1), F32)
    key_gate = []
    for r in range(n_pages):
        cum_page = cum_ref[0, pt_ref[sample, n_pages - 1 - r]]
        reach = reach + cum_page[:, page - 1:page]
        key_gate.append(cum_page - reach)
    key_gate = _dot_hi(expand, jnp.concatenate(key_gate, axis=1))
    for r in range(n_pages):
        tiles.append(_dot(qbd, k_refs[r][0, 0].astype(BF16)) - key_gate[:, r * page:(r + 1) * page])

    m = cq_row + functools.reduce(jnp.maximum, [jnp.max(t, axis=1, keepdims=True) for t in tiles])
    shift = cq_row - m
    p_new = jnp.exp(tiles[0] + shift)
    l = jnp.sum(p_new, axis=1, keepdims=True)
    acc = _dot(p_new.astype(BF16), vn_ref[0].astype(BF16))
    for r in range(n_pages):
        p = jnp.exp(tiles[r + 1] + shift)
        l = l + jnp.sum(p, axis=1, keepdims=True)
        acc = acc + _dot_nt(p.astype(BF16), v_refs[r][0, 0].astype(BF16))
    full = jnp.where(headmask, acc / l, 0.0)
    o_ref[0] = jnp.sum(full.reshape(FOX_HEADS, ls, FOX_WIDTH), axis=0)


def _fox_decode(layer, page_table, qb, k_new, v_new, sm, b_pad, cache_k, cache_v, cache_cum):
    bs, ls, _ = qb.shape
    n_pages = page_table.shape[1]
    page = cache_k.shape[3]

    def page_map(r):
        return lambda b, pt: (layer, pt[b, n_pages - 1 - r], 0, 0)

    per_sample = lambda n: pl.BlockSpec((1, ls, n), lambda b, pt: (b, 0, 0))
    in_specs = [per_sample(FOX_WIDTH), per_sample(FOX_WIDTH), per_sample(FOX_WIDTH), per_sample(LANES),
                pl.BlockSpec((1, LANES), lambda b, pt: (0, 0))]
    in_specs += [pl.BlockSpec((1, 1, FOX_WIDTH, page), page_map(r)) for r in range(n_pages)]
    in_specs += [pl.BlockSpec((1, 1, FOX_WIDTH, page), page_map(r)) for r in range(n_pages)]
    in_specs += [pl.BlockSpec((1,) + cache_cum.shape[1:], lambda b, pt: (layer, 0, 0, 0))]
    grid_spec = pltpu.PrefetchScalarGridSpec(
        num_scalar_prefetch=1, grid=(bs,), in_specs=in_specs,
        out_specs=pl.BlockSpec((1, ls, FOX_WIDTH), lambda b, pt: (b, 0, 0)))
    kern = functools.partial(_fox_decode_kernel, n_pages=n_pages, page=page, ls=ls)
    args = [page_table, qb, k_new, v_new, sm, b_pad]
    args += [cache_k] * n_pages + [cache_v] * n_pages + [cache_cum]
    return pl.pallas_call(
        kern, grid_spec=grid_spec, out_shape=jax.ShapeDtypeStruct((bs, ls, FOX_WIDTH), F32),
        compiler_params=_params("arbitrary"), name="fox_decode")(*args)


INV_BASE = 8
GDN_SUB_TILE = 128


def _bf(x):
    return x.astype(BF16)


def _mm(a, b):
    return _dot(_bf(a), _bf(b))


def _split3(x):
    hi = _bf(x)
    r1 = x - hi.astype(F32)
    mid = _bf(r1)
    return hi, mid, _bf(r1 - mid.astype(F32))


def _mm_near_f32(a, b):
    ah, al, _ = _split3(a)
    bh, bl, _ = _split3(b)
    return _dot(ah, bh) + (_dot(ah, bl) + _dot(al, bh))


def _gdn_gates(sm, smt, alog_row, dtb_row, alog_col, dtb_col, cs, sub):
    t_rows = sm.shape[0]
    lg = _log2(cs)
    r = _iota((sub, sub), 0)
    c = _iota((sub, sub), 1)
    same = (r >> lg) == (c >> lg)
    incl = same & (c <= r)
    strict = same & (c < r)
    base = min(INV_BASE, cs)
    base_blk = strict & ((r >> _log2(base)) == (c >> _log2(base)))
    merges = []
    s = base
    while s < cs:
        ls_ = _log2(s)
        merges.append(((r >> (ls_ + 1)) == (c >> (ls_ + 1))) & (((r >> ls_) & 1) == 1) & (((c >> ls_) & 1) == 0))
        s *= 2
    g_col = -jnp.exp(alog_row) * _softplus(sm + dtb_row)
    g_row = -jnp.exp(alog_col) * _softplus(smt + dtb_col)
    incl_b, same_b, upto_b = _bf(incl), _bf(same), _bf(same & (r <= c))
    beta = jax.nn.sigmoid(sm)
    subs = [slice(i, i + sub) for i in range(0, t_rows, sub)]
    gc_col = [sum(_dot(incl_b, part) for part in _split3(g_col[rows])) for rows in subs]
    gl_col = [sum(_dot(same_b, part) for part in _split3(g_col[rows])) for rows in subs]
    gc_row = [sum(_dot(part, upto_b) for part in _split3(g_row[:, rows])) for rows in subs]
    return dict(incl=incl, strict=strict, eye=(r == c).astype(F32), base_blk=base_blk, merges=merges, base=base,
                subs=subs, gc_col=gc_col, gl_col=gl_col, gc_row=gc_row, beta=[beta[rows] for rows in subs])


def _each(f, *lists):
    return [f(*args) for args in zip(*lists)]


def _unit_lower_inverses(a_list, gt):
    powers = _each(lambda a: jnp.where(gt["base_blk"], -a, 0.0), a_list)
    invs = _each(lambda n0: gt["eye"] + n0, powers)
    for _ in range(_log2(gt["base"]) - 1):
        powers = _each(lambda p: _mm(p, p), powers)
        invs = _each(lambda inv, p: inv + _mm(inv, p), invs, powers)
    for merge in gt["merges"]:
        invs_b = _each(_bf, invs)
        lower = _each(lambda inv_b, a: _bf(_dot(inv_b, _bf(jnp.where(merge, a, 0.0)))), invs_b, a_list)
        invs = _each(lambda inv, low, inv_b: inv - _dot(low, inv_b), invs, lower, invs_b)
    return invs


def _gdn_tiles(conv, gt):
    hd = GDN_HEAD_DIM
    heads = range(GDN_HEADS)
    act = lambda base, h: _silu(conv(slice(base + h * hd, base + (h + 1) * hd)))
    l2 = lambda x: x * lax.rsqrt(jnp.sum(x * x, axis=-1, keepdims=True) + EPS)
    q_full = [l2(act(0, h)) * (hd ** -0.5) for h in heads]
    k_full = [l2(act(GDN_WIDTH, h)) for h in heads]
    v_full = [act(2 * GDN_WIDTH, h) for h in heads]
    units = [(s, h) for s in range(len(gt["subs"])) for h in heads]
    q = [q_full[h][gt["subs"][s]] for s, h in units]
    k = [k_full[h][gt["subs"][s]] for s, h in units]
    v = [v_full[h][gt["subs"][s]] for s, h in units]
    beta = [gt["beta"][s][:, GB_COL + h:GB_COL + h + 1] for s, h in units]
    gc = [gt["gc_col"][s][:, GA_COL + h:GA_COL + h + 1] for s, h in units]
    gl = [gt["gl_col"][s][:, GA_COL + h:GA_COL + h + 1] for s, h in units]
    gc_row = [gt["gc_row"][s][GA_COL + h:GA_COL + h + 1, :] for s, h in units]

    decay = _each(lambda c, r: jnp.exp(jnp.where(gt["incl"], c - r, -jnp.inf)), gc, gc_row)
    kb = _each(_bf, k)
    kk = _each(_dot_nt, kb, kb)
    qk = _each(lambda q_, kb_, d: _dot_nt(_bf(q_), kb_) * d, q, kb, decay)
    a = _each(lambda kk_, d, b: jnp.where(gt["strict"], kk_ * d * b, 0.0), kk, decay, beta)
    egc = _each(jnp.exp, gc)
    rhs = _each(lambda v_, k_, b, e: jnp.concatenate([v_ * b, k_ * (b * e)], axis=1), v, k, beta, egc)
    inv_b = _each(_bf, _unit_lower_inverses(a, gt))
    x = _each(lambda i, r: _dot(i, _bf(r)), inv_b, rhs)
    resid = _each(lambda r, x_, a_: _bf(r - (x_ + _mm_near_f32(a_, x_))), rhs, x, a)
    x = _each(lambda x_, i, r: x_ + _dot(i, r), x, inv_b, resid)
    q_dec = _each(lambda q_, e: q_ * e, q, egc)
    k_dec = _each(lambda k_, l_, c: k_ * jnp.exp(l_ - c), k, gl, gc)
    return [x_[:, :hd] for x_ in x], [x_[:, hd:] for x_ in x], qk, q_dec, k_dec, gl


def _gdn_prompt_kernel(qkv_ref, sm_ref, smt_ref, gg_ref, cw_ref, alog_row, dtb_row, alog_col, dtb_col, nw_ref,
                       go_ref, s_out_ref, xbuf_ref, s_ref, *, cs):
    t = pl.program_id(1)
    t_rows = qkv_ref.shape[0]

    @pl.when(t == 0)
    def _():
        xbuf_ref[0:SUBLANES, :] = jnp.zeros((SUBLANES, GDN_CONV_DIM), F32)
        s_ref[...] = jnp.zeros_like(s_ref)

    xbuf_ref[SUBLANES:, :] = qkv_ref[...]

    def conv(cols):
        acc = xbuf_ref[SUBLANES:SUBLANES + t_rows, cols] * cw_ref[CONV_WIDTH - 1:CONV_WIDTH, cols]
        for w in range(CONV_WIDTH - 1):
            lo = SUBLANES - (CONV_WIDTH - 1) + w
            acc = acc + xbuf_ref[lo:lo + t_rows, cols] * cw_ref[w:w + 1, cols]
        return acc

    sub = min(GDN_SUB_TILE, t_rows)
    gt = _gdn_gates(sm_ref[...], smt_ref[...], alog_row[...], dtb_row[...], alog_col[...], dtb_col[...], cs, sub)
    hd = GDN_HEAD_DIM
    heads = range(GDN_HEADS)
    u_base, k_cum, qk, q_dec, k_dec, gl = _gdn_tiles(conv, gt)
    states = [s_ref[h] for h in heads]
    for s in range(t_rows // sub):
        unit = lambda h: s * GDN_HEADS + h
        us = [[] for _ in heads]
        ois = [[] for _ in heads]
        for ci in range(sub // cs):
            rows = slice(ci * cs, (ci + 1) * cs)
            sb = _each(_bf, states)
            u = [u_base[unit(h)][rows] - _dot(_bf(k_cum[unit(h)][rows]), sb[h]) for h in heads]
            for h in heads:
                ois[h].append(_dot(_bf(q_dec[unit(h)][rows]), sb[h]))
                us[h].append(u[h])
            states = [states[h] * jnp.exp(gl[unit(h)][ci * cs:ci * cs + 1, :])
                      + _dot_tn(_bf(k_dec[unit(h)][rows]), _bf(u[h])) for h in heads]
        for h in heads:
            o = jnp.concatenate(ois[h], axis=0) + _dot(_bf(qk[unit(h)]), _bf(jnp.concatenate(us[h], axis=0)))
            cols = slice(h * hd, (h + 1) * hd)
            go_ref[gt["subs"][s], cols] = _rms(o, nw_ref[...]) * _silu(gg_ref[gt["subs"][s], cols])
    for h in heads:
        s_ref[h] = states[h]

    xbuf_ref[0:SUBLANES, :] = qkv_ref[t_rows - SUBLANES:, :]

    @pl.when(t == pl.num_programs(1) - 1)
    def _():
        s_out_ref[0] = s_ref[...]


def _gdn_prompt(qkv, sm, smt, gg, cw, alog_row, dtb_row, alog_col, dtb_col, nw, batch, tile):
    m = qkv.shape[0]
    steps = m // batch // tile
    row = lambda n: pl.BlockSpec((tile, n), lambda b, t: (b * steps + t, 0))
    full = lambda a: pl.BlockSpec(a.shape, lambda b, t: (0,) * a.ndim)
    hd = GDN_HEAD_DIM
    return pl.pallas_call(
        functools.partial(_gdn_prompt_kernel, cs=math.gcd(tile, GDN_CHUNK)), grid=(batch, steps),
        in_specs=[row(GDN_CONV_DIM), row(LANES), pl.BlockSpec((SMALL_COLS, tile), lambda b, t: (0, b * steps + t)),
                  row(GDN_WIDTH), full(cw), full(alog_row), full(dtb_row), full(alog_col), full(dtb_col), full(nw)],
        out_specs=(row(GDN_WIDTH), pl.BlockSpec((1, GDN_HEADS, hd, hd), lambda b, t: (b, 0, 0, 0))),
        out_shape=(jax.ShapeDtypeStruct((m, GDN_WIDTH), F32),
                   jax.ShapeDtypeStruct((batch, GDN_HEADS, hd, hd), F32)),
        scratch_shapes=[pltpu.VMEM((tile + SUBLANES, GDN_CONV_DIM), F32), pltpu.VMEM((GDN_HEADS, hd, hd), F32)],
        compiler_params=_params("arbitrary", "arbitrary"), name="gdn_prompt")(
            qkv, sm, smt, gg, cw, alog_row, dtb_row, alog_col, dtb_col, nw)


def _gdn_sample_kernel(qkv_ref, hist_ref, sm_ref, smt_ref, gg_ref, s0_ref, cw_ref, alog_row, dtb_row,
                       alog_col, dtb_col, nw_ref, *rest, ls):
    go_ref, s_out_ref, xbuf_ref = rest[-3:]
    bt = qkv_ref.shape[0]
    xbuf_ref[:, 0:SUBLANES, :] = hist_ref[...]
    xbuf_ref[:, SUBLANES:, :] = qkv_ref[...]

    def conv(cols):
        acc = xbuf_ref[:, SUBLANES:SUBLANES + ls, cols] * cw_ref[CONV_WIDTH - 1:CONV_WIDTH, cols]
        for w in range(CONV_WIDTH - 1):
            lo = SUBLANES - (CONV_WIDTH - 1) + w
            acc = acc + xbuf_ref[:, lo:lo + ls, cols] * cw_ref[w:w + 1, cols]
        return acc.reshape(bt * ls, acc.shape[-1])

    gt = _gdn_gates(sm_ref[...].reshape(bt * ls, LANES), smt_ref[...], alog_row[...], dtb_row[...],
                    alog_col[...], dtb_col[...], ls, bt * ls)
    u_base, k_cum, qk, q_dec, k_dec, gl = _gdn_tiles(conv, gt)
    for h in range(GDN_HEADS):
        rows = [slice(bi * ls, (bi + 1) * ls) for bi in range(bt)]
        states = [s0_ref[0, bi, h] for bi in range(bt)]
        us = [u_base[h][r] - _dot(k_cum[h][r], s) for r, s in zip(rows, states)]
        ois = [_dot(q_dec[h][r], s) for r, s in zip(rows, states)]
        for bi, (r, s, u) in enumerate(zip(rows, states, us)):
            s_out_ref[0, bi, h] = s * jnp.exp(gl[h][bi * ls:bi * ls + 1, :]) + _dot_tn(k_dec[h][r], u)
        o = jnp.concatenate(ois, axis=0) + _dot(_bf(qk[h]), _bf(jnp.concatenate(us, axis=0)))
        hd = GDN_HEAD_DIM
        cols = slice(h * hd, (h + 1) * hd)
        gate = _silu(gg_ref[:, :, cols].reshape(bt * ls, hd))
        go_ref[:, :, cols] = (_rms(o, nw_ref[...]) * gate).reshape(bt, ls, hd)


def _gdn_sample(layer, qkv, hist, sm, smt, gg, s0_all, cw, alog_row, dtb_row, alog_col, dtb_col, nw, prev_states, bt):
    bs, ls, _ = qkv.shape
    per = lambda n: pl.BlockSpec((bt, ls, n), lambda i: (i, 0, 0))
    full = lambda a: pl.BlockSpec(a.shape, lambda i: (0,) * a.ndim)
    hd = GDN_HEAD_DIM
    st = pl.BlockSpec((1, bt, GDN_HEADS, hd, hd), lambda i: (layer, i, 0, 0, 0))
    args = [qkv, hist, sm, smt, gg, s0_all, cw, alog_row, dtb_row, alog_col, dtb_col, nw]
    in_specs = [per(GDN_CONV_DIM), pl.BlockSpec((bt, SUBLANES, GDN_CONV_DIM), lambda i: (i, 0, 0)), per(LANES),
                pl.BlockSpec((SMALL_COLS, bt * ls), lambda i: (0, i)), per(GDN_WIDTH), st,
                full(cw), full(alog_row), full(dtb_row), full(alog_col), full(dtb_col), full(nw)]
    aliases = {}
    if prev_states is not None:
        aliases = {len(args): 1}
        args.append(prev_states)
        in_specs.append(pl.BlockSpec(memory_space=pl.ANY))
    return pl.pallas_call(
        functools.partial(_gdn_sample_kernel, ls=ls), grid=(bs // bt,), in_specs=in_specs,
        out_specs=(per(GDN_WIDTH), st),
        out_shape=(jax.ShapeDtypeStruct((bs, ls, GDN_WIDTH), F32), jax.ShapeDtypeStruct(s0_all.shape, F32)),
        scratch_shapes=[pltpu.VMEM((bt, SUBLANES + ls, GDN_CONV_DIM), F32)],
        input_output_aliases=aliases, compiler_params=_params("arbitrary"), name="gdn_sample")(*args)


def _norm_matmul_kernel(x_ref, g_ref, w_ref, *o_refs):
    xn = _rms(x_ref[...], g_ref[...]).astype(BF16)
    n = w_ref.shape[1] // len(o_refs)
    for i, o_ref in enumerate(o_refs):
        o_ref[...] = _dot(xn, w_ref[:, i * n:(i + 1) * n])


def _norm_matmul(x, g, w, n_out, tm):
    m = x.shape[0]
    n = w.shape[1] // n_out
    return pl.pallas_call(
        _norm_matmul_kernel, grid=(m // tm,),
        in_specs=[pl.BlockSpec((tm, x.shape[1]), lambda i: (i, 0)), pl.BlockSpec(g.shape, lambda i: (0, 0)),
                  pl.BlockSpec(w.shape, lambda i: (0, 0))],
        out_specs=tuple(pl.BlockSpec((tm, n), lambda i: (i, 0)) for _ in range(n_out)),
        out_shape=tuple(jax.ShapeDtypeStruct((m, n), F32) for _ in range(n_out)),
        compiler_params=_params("arbitrary"), name="norm_matmul")(x, g, w)


def _matmul_residual_kernel(*refs, n_in):
    x_ref, o_ref = refs[0], refs[-1]
    acc = x_ref[...]
    for a_ref, w_ref in zip(refs[1:1 + n_in], refs[1 + n_in:1 + 2 * n_in]):
        acc = acc + _dot(a_ref[...].astype(BF16), w_ref[...])
    o_ref[...] = acc


def _matmul_residual(x, a_list, w_list, tm):
    m, d = x.shape
    n_in = len(a_list)
    in_specs = [pl.BlockSpec((tm, d), lambda i: (i, 0))]
    in_specs += [pl.BlockSpec((tm, a.shape[1]), lambda i: (i, 0)) for a in a_list]
    in_specs += [pl.BlockSpec(w.shape, lambda i: (0, 0)) for w in w_list]
    return pl.pallas_call(
        functools.partial(_matmul_residual_kernel, n_in=n_in), grid=(m // tm,), in_specs=in_specs,
        out_specs=pl.BlockSpec((tm, d), lambda i: (i, 0)), out_shape=jax.ShapeDtypeStruct((m, d), F32),
        compiler_params=_params("arbitrary"), name="matmul_residual")(x, *a_list, *w_list)


def _mem_attn_prompt_kernel(x_ref, g_ref, wq_ref, wo_ref, mk_ref, mv_ref, o_ref):
    x = x_ref[0]
    xn = _rms(x, g_ref[...]).astype(BF16)
    q = _dot(xn, wq_ref[...]).astype(BF16)
    cols = [slice(h * MEM_HEAD_DIM, (h + 1) * MEM_HEAD_DIM) for h in range(MEM_HEADS)]
    s = [_dot_nt(q[:, c], mk_ref[0, :, c].astype(BF16)) * (MEM_HEAD_DIM ** -0.5) for c in cols]
    p = [jnp.exp(s_ - jnp.max(s_, axis=1, keepdims=True)) for s_ in s]
    o = [_dot(p_.astype(BF16), mv_ref[0, :, c].astype(BF16)) for p_, c in zip(p, cols)]
    outs = [o_ / jnp.sum(p_, axis=1, keepdims=True) for o_, p_ in zip(o, p)]
    o_ref[0] = x + _dot(jnp.concatenate(outs, axis=1).astype(BF16), wo_ref[...])


def _mem_attn_prompt(x, g, wq, wo, mk, mv, tm):
    b, l, d = x.shape
    n_mem = mk.shape[1]
    full = lambda a: pl.BlockSpec(a.shape, lambda bi, i: (0,) * a.ndim)
    mem_spec = pl.BlockSpec((1, n_mem, MEM_WIDTH), lambda bi, i: (bi, 0, 0))
    x_spec = pl.BlockSpec((1, tm, d), lambda bi, i: (bi, i, 0))
    return pl.pallas_call(
        _mem_attn_prompt_kernel, grid=(b, l // tm),
        in_specs=[x_spec, full(g), full(wq), full(wo), mem_spec, mem_spec],
        out_specs=x_spec, out_shape=jax.ShapeDtypeStruct(x.shape, F32),
        compiler_params=_params("arbitrary", "arbitrary"), name="mem_attn_prompt")(x, g, wq, wo, mk, mv)


def _mem_attn_sample_kernel(q_ref, mk_ref, mv_ref, o_ref):
    bt, ls, _ = q_ref.shape
    rows = MEM_HEADS * ls
    n = mk_ref.shape[2]
    hd = MEM_HEAD_DIM
    own_head = _mod(_iota((rows, n), 1), MEM_HEADS) == _div(_iota((rows, n), 0), ls)
    for bi in range(bt):
        q = q_ref[bi]
        qs = jnp.concatenate([q[:, h * hd:(h + 1) * hd] for h in range(MEM_HEADS)], axis=0)
        s = _dot_nt(qs.astype(BF16), mk_ref[0, bi].astype(BF16)) * (hd ** -0.5)
        s = jnp.where(own_head, s, -jnp.inf)
        p = jnp.exp(s - jnp.max(s, axis=1, keepdims=True))
        o = _dot(p.astype(BF16), mv_ref[0, bi].astype(BF16)) / jnp.sum(p, axis=1, keepdims=True)
        o_ref[bi] = jnp.concatenate([o[h * ls:(h + 1) * ls] for h in range(MEM_HEADS)], axis=1)


def _mem_attn_sample(layer, q, mk_all, mv_all, bt):
    bs, ls, _ = q.shape
    mem_spec = pl.BlockSpec((1, bt) + mk_all.shape[2:], lambda i: (layer, i, 0, 0))
    q_spec = pl.BlockSpec((bt, ls, MEM_WIDTH), lambda i: (i, 0, 0))
    return pl.pallas_call(
        _mem_attn_sample_kernel, grid=(bs // bt,), in_specs=[q_spec, mem_spec, mem_spec],
        out_specs=q_spec, out_shape=jax.ShapeDtypeStruct(q.shape, F32),
        compiler_params=_params("arbitrary"), name="mem_attn_sample")(q, mk_all, mv_all)


def _ffn_kernel(x_ref, g_ref, wa_ref, wu_ref, wo_ref, gf_ref, o_ref, *, chunk, final_norm):
    x = x_ref[...]
    xn = _rms(x, g_ref[...]).astype(BF16)
    n = wa_ref.shape[1] // chunk
    acc = x
    act = None
    for c in range(n + 1):
        if c < n:
            cols = slice(c * chunk, (c + 1) * chunk)
            a = _dot(xn, wa_ref[:, cols])
            u = _dot(xn, wu_ref[:, cols])
        if c > 0:
            acc = acc + _dot(act, wo_ref[(c - 1) * chunk:c * chunk, :])
        if c < n:
            act = (_silu(a) * u).astype(BF16)
    if final_norm:
        acc = _rms(acc, gf_ref[...])
    o_ref[...] = acc


def _ffn(x, g, wa, wu, wo, gf, final_norm, tm):
    m, d = x.shape
    full = lambda a: pl.BlockSpec(a.shape, lambda i: (0,) * a.ndim)
    row = pl.BlockSpec((tm, d), lambda i: (i, 0))
    return pl.pallas_call(
        functools.partial(_ffn_kernel, chunk=2 * LANES, final_norm=final_norm), grid=(m // tm,),
        in_specs=[row, full(g), full(wa), full(wu), full(wo), full(gf)], out_specs=row,
        out_shape=jax.ShapeDtypeStruct((m, d), F32), compiler_params=_params("arbitrary"), name="ffn")(
            x, g, wa, wu, wo, gf)


def _pick(m, cap):
    t = min(m, cap)
    while m % t:
        t -= SUBLANES
    assert t > 0 and m % t == 0, (m, cap)
    return t


def _pad_lanes(v, offset):
    row = jnp.zeros((1, LANES), F32).at[0, offset:offset + v.shape[0]].set(v)
    return row, row[0, :SMALL_COLS].reshape(SMALL_COLS, 1)


def kernel(x_prompt, x_sample, cache_fox_k, cache_fox_v, cache_fox_logf, state_gdn, state_gdn_conv, cache_mem_k, cache_mem_v, page_table, mem_prompt, g_norm_mix, w_in, b_fox_f, gdn_conv_w, gdn_a_log, gdn_dt_bias, gdn_norm_w, w_out, g_norm_memin, w_mem_kv, g_norm_mem, w_mem_q, w_mem_o, g_norm_ffn, w_ffn_in, w_ffn_out, g_final):
    bp, lp, d = x_prompt.shape
    bs, ls, _ = x_sample.shape
    depth = w_in.shape[0]
    n_phys, page = cache_fox_k.shape[1], cache_fox_k.shape[2]
    n_pages = page_table.shape[1]
    n_mem = mem_prompt.shape[1]
    d_ff = w_ffn_out.shape[1]
    mp, ms = bp * lp, bs * ls
    hd = GDN_HEAD_DIM

    yp = x_prompt.reshape(mp, d)
    ys = x_sample.reshape(ms, d)
    mem_flat = mem_prompt.reshape(bp * n_mem, d)
    cache_k = cache_fox_k.transpose(0, 1, 3, 4, 2).reshape(depth, n_phys, FOX_WIDTH, page)
    cache_v = cache_fox_v.transpose(0, 1, 3, 4, 2).reshape(depth, n_phys, FOX_WIDTH, page)
    page_rows = depth * n_phys * FOX_HEADS
    cache_cum = _cumsum_pages(cache_fox_logf.transpose(0, 1, 3, 2).reshape(page_rows, page), _pick(page_rows, 2048))
    cache_cum = cache_cum.reshape(depth, n_phys, FOX_HEADS, page)
    tokens_per_row = LANES // FOX_HEADS
    hist = jnp.pad(state_gdn_conv, ((0, 0), (0, 0), (SUBLANES - (CONV_WIDTH - 1), 0), (0, 0)))
    mem_k_cache = cache_mem_k.reshape(depth, bs, n_mem * MEM_HEADS, MEM_HEAD_DIM)
    mem_v_cache = cache_mem_v.reshape(depth, bs, n_mem * MEM_HEADS, MEM_HEAD_DIM)

    o1 = FOX_WIDTH * 3
    o2 = o1 + FOX_HEADS
    o3 = o2 + GDN_CONV_DIM
    o4 = o3 + 2 * GDN_HEADS
    row2 = lambda v: v.reshape(1, -1)

    outs_p = {k: [] for k in ("lf", "gs", "gc", "mk", "mv")}
    outs_s = {k: [] for k in ("fk", "fv", "lf", "gc")}
    kt_all = vt_all = states_s = None
    for l in range(depth):
        w = w_in[l]
        wq = w[:, :FOX_WIDTH].astype(BF16)
        wqt = w[:, :FOX_WIDTH].T.astype(BF16)
        wkv = w[:, FOX_WIDTH:o1].astype(BF16)
        wk = w[:, FOX_WIDTH:2 * FOX_WIDTH].astype(BF16)
        wkvt = w[:, FOX_WIDTH:o1].T.astype(BF16)
        wg = jnp.concatenate([w[:, o2:o3], w[:, o4:]], axis=1).astype(BF16)
        w_small = jnp.concatenate([w[:, o1:o2], w[:, o3:o4]], axis=1)
        ws = jnp.pad(w_small, ((0, 0), (0, LANES - SMALL_COLS))).astype(BF16)
        wst = w_small.T.astype(BF16)
        b_row = row2(b_fox_f[l])
        b_col = b_fox_f[l].reshape(FOX_HEADS, 1)
        b_pad = jnp.pad(b_row, ((0, 0), (0, LANES - FOX_HEADS)))
        alog_row, alog_col = _pad_lanes(gdn_a_log[l], GA_COL)
        dtb_row, dtb_col = _pad_lanes(gdn_dt_bias[l], GA_COL)
        cw = gdn_conv_w[l]
        nw = row2(gdn_norm_w[l])
        wo_fox = w_out[l][:FOX_WIDTH].astype(BF16)
        wo_gdn = w_out[l][FOX_WIDTH:].astype(BF16)
        w_kv = w_mem_kv[l].astype(BF16)
        w_q = w_mem_q[l].astype(BF16)
        w_o = w_mem_o[l].astype(BF16)
        wa = w_ffn_in[l][:, :d_ff].astype(BF16)
        wu = w_ffn_in[l][:, d_ff:].astype(BF16)
        wf = w_ffn_out[l].astype(BF16)
        g_mix, g_memin, g_mem, g_ffn = row2(g_norm_mix[l]), row2(g_norm_memin[l]), row2(g_norm_mem[l]), row2(g_norm_ffn[l])
        gf = row2(g_final)
        last = l == depth - 1

        kt_all, vt_all, kaug, qtaug, vtaug, qkv, gg, lft, sm, smt = _in_proj_prompt(
            l, depth, bp, yp, g_mix, wqt, wk, wkvt, wg, ws, wst, b_row, b_col, kt_all, vt_all, _pick(lp, 256))
        fo = _fox_prompt(kaug, qtaug, vtaug, _pick(lp, 512)).reshape(mp, FOX_WIDTH)
        go, s_new = _gdn_prompt(qkv, sm, smt, gg, cw, alog_row, dtb_row, alog_col, dtb_col, nw, bp, _pick(lp, 256))
        yp = _matmul_residual(yp, [fo, go], [wo_fox, wo_gdn], _pick(mp, 512))
        mk, mv = _norm_matmul(mem_flat, g_memin, w_kv, 2, _pick(bp * n_mem, 256))
        yp = _mem_attn_prompt(yp.reshape(bp, lp, d), g_mem, w_q, w_o, mk.reshape(bp, n_mem, MEM_WIDTH),
                              mv.reshape(bp, n_mem, MEM_WIDTH), _pick(lp, 512)).reshape(mp, d)
        yp = _ffn(yp, g_ffn, wa, wu, wf, gf, last, _pick(mp, 256))
        outs_p["lf"].append(lft.transpose(0, 2, 1))
        outs_p["gs"].append(s_new)
        outs_p["gc"].append(qkv.reshape(bp, lp, GDN_CONV_DIM)[:, lp - (CONV_WIDTH - 1):])
        outs_p["mk"].append(mk.reshape(bp, n_mem, MEM_HEADS, MEM_HEAD_DIM))
        outs_p["mv"].append(mv.reshape(bp, n_mem, MEM_HEADS, MEM_HEAD_DIM))

        fqb, fk, fv, qkv, gg, lf, sm, smt = _in_proj_sample(ys, g_mix, wq, wkv, wg, ws, wst, b_row, _pick(ms, 256))
        s3 = lambda a: a.reshape(bs, ls, a.shape[-1])
        fo = _fox_decode(l, page_table, s3(fqb), s3(fk), s3(fv), s3(sm), b_pad, cache_k, cache_v,
                         cache_cum).reshape(ms, FOX_WIDTH)
        go, states_s = _gdn_sample(l, s3(qkv), hist[l], s3(sm), smt, s3(gg), state_gdn, cw, alog_row, dtb_row,
                                   alog_col, dtb_col, nw, states_s, _pick(bs, 16))
        ys = _matmul_residual(ys, [fo, go.reshape(ms, GDN_WIDTH)], [wo_fox, wo_gdn], _pick(ms, 512))
        (q_mem,) = _norm_matmul(ys, g_mem, w_q, 1, _pick(ms, 256))
        o_mem = _mem_attn_sample(l, s3(q_mem), mem_k_cache, mem_v_cache, _pick(bs, 8)).reshape(ms, MEM_WIDTH)
        ys = _matmul_residual(ys, [o_mem], [w_o], _pick(ms, 512))
        ys = _ffn(ys, g_ffn, wa, wu, wf, gf, last, _pick(ms, 256))
        outs_s["fk"].append(fk.reshape(bs, ls, FOX_HEADS, FOX_HEAD_DIM))
        outs_s["fv"].append(fv.reshape(bs, ls, FOX_HEADS, FOX_HEAD_DIM))
        outs_s["lf"].append(lf.reshape(bs, ls, FOX_HEADS))
        outs_s["gc"].append(s3(qkv)[:, ls - (CONV_WIDTH - 1):])

    st = jnp.stack
    heads_last = lambda a: a.reshape(depth, bp, FOX_HEADS, FOX_HEAD_DIM, lp).transpose(0, 1, 4, 2, 3)
    return (yp.reshape(bp, lp, d), ys.reshape(bs, ls, d),
            heads_last(kt_all), heads_last(vt_all), st(outs_p["lf"]), st(outs_p["gs"]), st(outs_p["gc"]),
            st(outs_p["mk"]), st(outs_p["mv"]),
            st(outs_s["fk"]), st(outs_s["fv"]), st(outs_s["lf"]), states_s, st(outs_s["gc"]))
```

```python
import functools
import math

import jax
import jax.numpy as jnp
from jax import lax
from jax.experimental import pallas as pl
from jax.experimental.pallas import tpu as pltpu

F32 = jnp.float32
BF16 = jnp.bfloat16
HI = lax.Precision.HIGHEST

EPS = 1e-6
D_MODEL = 1024
FOX_HEADS = 8
FOX_HEAD_DIM = 64
FOX_WIDTH = FOX_HEADS * FOX_HEAD_DIM
GDN_HEADS = 4
GDN_HEAD_DIM = 128
GDN_WIDTH = GDN_HEADS * GDN_HEAD_DIM
GDN_CONV_DIM = 3 * GDN_WIDTH
CONV_WIDTH = 4
GDN_CHUNK = 64
MEM_HEADS = 4
MEM_HEAD_DIM = 128
MEM_WIDTH = MEM_HEADS * MEM_HEAD_DIM
LANES = 128
SUBLANES = 8
SMALL_COLS = 16
GA_COL = FOX_HEADS
GB_COL = FOX_HEADS + GDN_HEADS
VMEM_LIMIT = 56 * 1024 * 1024


def _params(*sem):
    return pltpu.CompilerParams(dimension_semantics=sem, vmem_limit_bytes=VMEM_LIMIT)


def _dot(a, b):
    return jnp.dot(a, b, preferred_element_type=F32)


def _dot_nt(a, b):
    return lax.dot_general(a, b, (((1,), (1,)), ((), ())), preferred_element_type=F32)


def _dot_tn(a, b):
    return lax.dot_general(a, b, (((0,), (0,)), ((), ())), preferred_element_type=F32)


def _dot_hi(a, b):
    return jnp.dot(a, b, precision=HI, preferred_element_type=F32)


def _dot_nt_hi(a, b):
    return lax.dot_general(a, b, (((1,), (1,)), ((), ())), precision=HI, preferred_element_type=F32)


def _rms(x, g):
    return x * lax.rsqrt(jnp.mean(x * x, axis=-1, keepdims=True) + EPS) * g


def _log_sigmoid(x):
    return jnp.minimum(x, 0.0) - jnp.log1p(jnp.exp(-jnp.abs(x)))


def _softplus(x):
    return jnp.maximum(x, 0.0) + jnp.log1p(jnp.exp(-jnp.abs(x)))


def _silu(x):
    return x * jax.nn.sigmoid(x)


def _iota(shape, dim):
    return lax.broadcasted_iota(jnp.int32, shape, dim)


def _log2(n):
    lg = n.bit_length() - 1
    assert n == 1 << lg, n
    return lg


def _div(x, n):
    return x >> _log2(n)


def _mod(x, n):
    _log2(n)
    return x & (n - 1)


def _in_proj_common(x_ref, g_ref, wq_ref, wg_ref, ws_ref, wst_ref, b_ref,
                    fqb_ref, qkv_ref, gg_ref, lf_ref, sm_ref, smt_ref):
    xn = _rms(x_ref[...], g_ref[...]).astype(BF16)
    fqb_ref[...] = (_dot(xn, wq_ref[...]) * (FOX_HEAD_DIM ** -0.5)).astype(BF16)
    qkv_ref[...] = _dot(xn, wg_ref[:, 0:GDN_CONV_DIM])
    gg_ref[...] = _dot(xn, wg_ref[:, GDN_CONV_DIM:])
    sm = _dot(xn, ws_ref[...])
    smt = _dot_nt(wst_ref[...], xn)
    sm_ref[...] = sm
    smt_ref[...] = smt
    lf_ref[...] = _log_sigmoid(sm[:, 0:FOX_HEADS] + b_ref[...])
    return xn, smt


LOG2E = 1.4426950408889634
AUG_ROWS = SUBLANES
VALUE_ROWS = FOX_HEAD_DIM + 2 * SUBLANES


def _in_proj_prompt_kernel(x_ref, g_ref, wqt_ref, wk_ref, wkvt_ref, wg_ref, ws_ref, wst_ref, b_ref, bcol_ref, cw_ref,
                           *rest, tiles_per_seq):
    (kt_ref, vt_ref, kaug_ref, qtaug_ref, vtaug_ref, qkv_ref, gg_ref, tail_ref, lft_ref, sm_ref, smt_ref,
     crow_ref, ccol_ref, xbuf_ref) = rest[-14:]
    tm = x_ref.shape[0]
    hd = FOX_HEAD_DIM

    @pl.when(pl.program_id(0) % tiles_per_seq == 0)
    def _():
        crow_ref[...] = jnp.zeros_like(crow_ref)
        ccol_ref[...] = jnp.zeros_like(ccol_ref)
        xbuf_ref[0:SUBLANES, :] = jnp.zeros((SUBLANES, GDN_CONV_DIM), F32)

    xn = _rms(x_ref[...], g_ref[...]).astype(BF16)
    sm = _dot(xn, ws_ref[...])
    smt = _dot_nt(wst_ref[...], xn)
    sm_ref[...] = sm
    smt_ref[...] = smt
    lf_col = _log_sigmoid(sm[:, 0:FOX_HEADS] + b_ref[...])
    lf_row = _log_sigmoid(smt[0:FOX_HEADS, :] + bcol_ref[...])
    lft_ref[0] = lf_row
    gdn_cols = wg_ref.shape[1] // FOX_HEADS

    def gdn_slice(i):
        cols = slice(i * gdn_cols, (i + 1) * gdn_cols)
        part = _dot(xn, wg_ref[:, cols])
        if (i + 1) * gdn_cols > GDN_CONV_DIM:
            gg_ref[:, i * gdn_cols - GDN_CONV_DIM:(i + 1) * gdn_cols - GDN_CONV_DIM] = part
            return
        xbuf_ref[SUBLANES:, cols] = part
        conv = part * cw_ref[CONV_WIDTH - 1:CONV_WIDTH, cols]
        for w in range(CONV_WIDTH - 1):
            lo = SUBLANES - (CONV_WIDTH - 1) + w
            conv = conv + xbuf_ref[lo:lo + tm, cols] * cw_ref[w:w + 1, cols]
        act = _silu(conv)
        for c0 in range(i * gdn_cols, (i + 1) * gdn_cols, GDN_HEAD_DIM):
            blk = act[:, c0 - i * gdn_cols:c0 - i * gdn_cols + GDN_HEAD_DIM]
            if c0 < 2 * GDN_WIDTH:
                blk = blk * lax.rsqrt(jnp.sum(blk * blk, axis=-1, keepdims=True) + EPS)
            if c0 < GDN_WIDTH:
                blk = blk * (GDN_HEAD_DIM ** -0.5)
            qkv_ref[:, c0:c0 + GDN_HEAD_DIM] = blk

    kt = _dot_nt(wkvt_ref[0:FOX_WIDTH, :], xn)
    kt_ref[0, 0] = kt
    gdn_slice(0)
    vt = _dot_nt(wkvt_ref[FOX_WIDTH:, :], xn)
    vt_ref[0, 0] = vt
    gdn_slice(1)
    k_tok = _dot(xn, wk_ref[...])
    gdn_slice(2)
    qt = _dot_nt(wqt_ref[...], xn) * (hd ** -0.5 * LOG2E)
    gdn_slice(3)

    r = _iota((tm, tm), 0)
    c = _iota((tm, tm), 1)
    c_col = sum(_dot(_bf(c <= r), part) for part in _split3(lf_col)) + crow_ref[...]
    c_row = sum(_dot(part, _bf(r <= c)) for part in _split3(lf_row)) + ccol_ref[...]
    crow_ref[...] = c_col[tm - 1:tm, :]
    ccol_ref[...] = c_row[:, tm - 1:tm]

    lane = _iota((tm, LANES), 1)
    sub = _iota((AUG_ROWS, tm), 0)
    ones_key = jnp.where((lane >= hd + 3) & (lane < hd + 6), 1.0, 0.0)
    zeros_tail = jnp.zeros((LANES - hd - AUG_ROWS, tm), F32)
    value_tail = jnp.where(_iota((VALUE_ROWS - hd, tm), 0) == 0, 1.0, 0.0)
    for h in range(FOX_HEADS):
        if h + FOX_HEADS // 2 < FOX_HEADS:
            gdn_slice(h + FOX_HEADS // 2)
        pair = k_tok[:, (h // 2) * LANES:(h // 2 + 1) * LANES]
        k_h = pair if h % 2 == 0 else pltpu.roll(pair, hd, 1)
        k1, k2, k3 = [p.astype(F32) for p in _split3(c_col[:, h:h + 1] * (-LOG2E))]
        extra = jnp.where(lane == hd, k1, jnp.where(lane == hd + 1, k2, jnp.where(lane == hd + 2, k3, ones_key)))
        kaug_ref[0, h] = jnp.where(lane < hd, k_h, extra).astype(BF16)
        q1, q2, q3 = [p.astype(F32) for p in _split3(c_row[h:h + 1, :] * LOG2E)]
        query_tail = jnp.where(sub < 3, 1.0, jnp.where(sub == 3, q1, jnp.where(sub == 4, q2,
                                                                           jnp.where(sub == 5, q3, 0.0))))
        rows = slice(h * hd, (h + 1) * hd)
        qtaug_ref[0, h] = jnp.concatenate([qt[rows], query_tail, zeros_tail], axis=0).astype(BF16)
        vtaug_ref[0, h] = jnp.concatenate([vt[rows], value_tail], axis=0).astype(BF16)
    last_rows = xbuf_ref[tm:tm + SUBLANES, :]
    tail_ref[0] = last_rows
    xbuf_ref[0:SUBLANES, :] = last_rows


def _in_proj_prompt(layer, depth, batch, x, g, wqt, wk, wkvt, wg, ws, wst, b_row, b_col, cw, prev_kt, prev_vt, tm):
    m = x.shape[0]
    l = m // batch
    nt = l // tm
    row = lambda n: pl.BlockSpec((tm, n), lambda i: (i, 0))
    full = lambda a: pl.BlockSpec(a.shape, lambda i: (0,) * a.ndim)
    kv_all = pl.BlockSpec((1, 1, FOX_WIDTH, tm), lambda i: (layer, i // nt, 0, i % nt))
    feat_major = pl.BlockSpec((1, FOX_HEADS, LANES, tm), lambda i: (i // nt, 0, 0, i % nt))
    out_shape = (
        jax.ShapeDtypeStruct((depth, batch, FOX_WIDTH, l), F32),
        jax.ShapeDtypeStruct((depth, batch, FOX_WIDTH, l), F32),
        jax.ShapeDtypeStruct((batch, FOX_HEADS, l, LANES), BF16),
        jax.ShapeDtypeStruct((batch, FOX_HEADS, LANES, l), BF16),
        jax.ShapeDtypeStruct((batch, FOX_HEADS, VALUE_ROWS, l), BF16),
        jax.ShapeDtypeStruct((m, GDN_CONV_DIM), F32),
        jax.ShapeDtypeStruct((m, GDN_WIDTH), F32),
        jax.ShapeDtypeStruct((batch, SUBLANES, GDN_CONV_DIM), F32),
        jax.ShapeDtypeStruct((batch, FOX_HEADS, l), F32),
        jax.ShapeDtypeStruct((m, LANES), F32),
        jax.ShapeDtypeStruct((SMALL_COLS, m), F32),
    )
    out_specs = (kv_all, kv_all,
                 pl.BlockSpec((1, FOX_HEADS, tm, LANES), lambda i: (i // nt, 0, i % nt, 0)), feat_major,
                 pl.BlockSpec((1, FOX_HEADS, VALUE_ROWS, tm), lambda i: (i // nt, 0, 0, i % nt)),
                 row(GDN_CONV_DIM), row(GDN_WIDTH),
                 pl.BlockSpec((1, SUBLANES, GDN_CONV_DIM), lambda i: (i // nt, 0, 0)),
                 pl.BlockSpec((1, FOX_HEADS, tm), lambda i: (i // nt, 0, i % nt)), row(LANES),
                 pl.BlockSpec((SMALL_COLS, tm), lambda i: (0, i)))
    args = [x, g, wqt, wk, wkvt, wg, ws, wst, b_row, b_col, cw]
    in_specs = [row(D_MODEL)] + [full(a) for a in args[1:]]
    aliases = {}
    if prev_kt is not None:
        aliases = {len(args): 0, len(args) + 1: 1}
        args += [prev_kt, prev_vt]
        in_specs += [pl.BlockSpec(memory_space=pl.ANY)] * 2
    return pl.pallas_call(
        functools.partial(_in_proj_prompt_kernel, tiles_per_seq=nt), grid=(m // tm,), in_specs=in_specs,
        out_specs=out_specs, out_shape=out_shape, input_output_aliases=aliases,
        scratch_shapes=[pltpu.VMEM((1, FOX_HEADS), F32), pltpu.VMEM((FOX_HEADS, 1), F32),
                        pltpu.VMEM((tm + SUBLANES, GDN_CONV_DIM), F32)],
        compiler_params=_params("arbitrary"), name="in_proj_prompt")(*args)


def _in_proj_sample_kernel(x_ref, g_ref, wq_ref, wkv_ref, wg_ref, ws_ref, wst_ref, b_ref,
                           fqb_ref, fk_ref, fv_ref, qkv_ref, gg_ref, lf_ref, sm_ref, smt_ref):
    xn, _ = _in_proj_common(x_ref, g_ref, wq_ref, wg_ref, ws_ref, wst_ref, b_ref,
                            fqb_ref, qkv_ref, gg_ref, lf_ref, sm_ref, smt_ref)
    fk_ref[...] = _dot(xn, wkv_ref[:, 0:FOX_WIDTH])
    fv_ref[...] = _dot(xn, wkv_ref[:, FOX_WIDTH:])


def _in_proj_sample(x, g, wq, wkv, wg, ws, wst, b_row, tm):
    m = x.shape[0]
    row = lambda n: pl.BlockSpec((tm, n), lambda i: (i, 0))
    full = lambda a: pl.BlockSpec(a.shape, lambda i: (0,) * a.ndim)
    out_shape = (
        jax.ShapeDtypeStruct((m, FOX_WIDTH), BF16),
        jax.ShapeDtypeStruct((m, FOX_WIDTH), F32),
        jax.ShapeDtypeStruct((m, FOX_WIDTH), F32),
        jax.ShapeDtypeStruct((m, GDN_CONV_DIM), F32),
        jax.ShapeDtypeStruct((m, GDN_WIDTH), F32),
        jax.ShapeDtypeStruct((m, FOX_HEADS), F32),
        jax.ShapeDtypeStruct((m, LANES), F32),
        jax.ShapeDtypeStruct((SMALL_COLS, m), F32),
    )
    out_specs = (row(FOX_WIDTH), row(FOX_WIDTH), row(FOX_WIDTH), row(GDN_CONV_DIM), row(GDN_WIDTH),
                 row(FOX_HEADS), row(LANES), pl.BlockSpec((SMALL_COLS, tm), lambda i: (0, i)))
    args = [x, g, wq, wkv, wg, ws, wst, b_row]
    return pl.pallas_call(
        _in_proj_sample_kernel, grid=(m // tm,), in_specs=[row(D_MODEL)] + [full(a) for a in args[1:]],
        out_specs=out_specs, out_shape=out_shape, compiler_params=_params("arbitrary"),
        name="in_proj_sample")(*args)


def _cumsum_pages_kernel(x_ref, o_ref):
    n = x_ref.shape[1]
    upto = _bf(_iota((n, n), 0) <= _iota((n, n), 1))
    o_ref[...] = sum(_dot(part, upto) for part in _split3(x_ref[...]))


def _cumsum_pages(x, block_rows):
    rows, n = x.shape
    spec = pl.BlockSpec((block_rows, n), lambda i: (i, 0))
    return pl.pallas_call(
        _cumsum_pages_kernel, grid=(rows // block_rows,), in_specs=[spec], out_specs=spec,
        out_shape=jax.ShapeDtypeStruct(x.shape, F32), compiler_params=_params("arbitrary"),
        name="cumsum_pages")(x)


FOX_UNIT_Q = 256
FOX_UNIT_GROUP = 8


def _fox_prompt_kernel(qi_ref, ki_ref, k_ref, qt_ref, vt_ref, o_ref, m_ref, acc_ref, *, tq, tk):
    step_id = pl.program_id(1)
    qi = qi_ref[step_id]
    ki = ki_ref[step_id]
    uq = min(FOX_UNIT_Q, tq)
    hd = FOX_HEAD_DIM

    @pl.when(ki == 0)
    def _():
        m_ref[...] = jnp.full_like(m_ref, -jnp.inf)
        acc_ref[...] = jnp.zeros_like(acc_ref)

    units = [(h, qc) for h in range(FOX_HEADS) for qc in range(tq // uq)]

    def step(diagonal):
        for g0 in range(0, len(units), FOX_UNIT_GROUP):
            group = units[g0:g0 + FOX_UNIT_GROUP]
            cols = [slice(qc * uq, (qc + 1) * uq) for _, qc in group]
            keys = [slice(0, (qc + 1) * uq if diagonal else tk) for _, qc in group]
            ts = [_dot(k_ref[0, h, ks], qt_ref[0, h, :, cs_]) for (h, _), cs_, ks in zip(group, cols, keys)]
            if diagonal:
                ts = [jnp.where(_iota(t.shape, 0) <= _iota(t.shape, 1) + qc * uq, t, -jnp.inf)
                      for t, (_, qc) in zip(ts, group)]
            m_old = [m_ref[h, :, cs_] for (h, _), cs_ in zip(group, cols)]
            m_new = [jnp.maximum(mo, jnp.max(t, axis=0, keepdims=True)) for mo, t in zip(m_old, ts)]
            ps = [jnp.exp2(t - mn).astype(BF16) for t, mn in zip(ts, m_new)]
            pvs = [_dot(vt_ref[0, h, :, ks], p) for (h, _), ks, p in zip(group, keys, ps)]
            for (h, _), cs_, mo, mn, pv in zip(group, cols, m_old, m_new, pvs):
                acc_ref[h, :, cs_] = jnp.exp2(mo - mn) * acc_ref[h, :, cs_] + pv
                m_ref[h, :, cs_] = mn

    @pl.when(ki < qi)
    def _():
        step(False)

    @pl.when(ki == qi)
    def _():
        step(True)
        for j in range(FOX_HEADS // 2):
            pair = [acc_ref[h, 0:hd, :] / acc_ref[h, hd:hd + 1, :] for h in (2 * j, 2 * j + 1)]
            o_ref[0, :, j * LANES:(j + 1) * LANES] = jnp.concatenate(pair, axis=0).T


def _fox_prompt(kaug, qtaug, vtaug, tq):
    b, _, l, _ = kaug.shape
    tk = tq
    n = l // tq
    pairs = [(qi, ki) for qi in range(n) for ki in range(qi + 1)]
    qi_tab = jnp.asarray([p[0] for p in pairs], jnp.int32)
    ki_tab = jnp.asarray([p[1] for p in pairs], jnp.int32)
    grid_spec = pltpu.PrefetchScalarGridSpec(
        num_scalar_prefetch=2, grid=(b, len(pairs)),
        in_specs=[pl.BlockSpec((1, FOX_HEADS, tk, LANES), lambda bi, s, qt, kt: (bi, 0, kt[s], 0)),
                  pl.BlockSpec((1, FOX_HEADS, LANES, tq), lambda bi, s, qt, kt: (bi, 0, 0, qt[s])),
                  pl.BlockSpec((1, FOX_HEADS, VALUE_ROWS, tk), lambda bi, s, qt, kt: (bi, 0, 0, kt[s]))],
        out_specs=pl.BlockSpec((1, tq, FOX_WIDTH), lambda bi, s, qt, kt: (bi, qt[s], 0)),
        scratch_shapes=[pltpu.VMEM((FOX_HEADS, 1, tq), F32), pltpu.VMEM((FOX_HEADS, VALUE_ROWS, tq), F32)])
    return pl.pallas_call(
        functools.partial(_fox_prompt_kernel, tq=tq, tk=tk), grid_spec=grid_spec,
        out_shape=jax.ShapeDtypeStruct((b, l, FOX_WIDTH), F32),
        compiler_params=_params("arbitrary", "arbitrary"), name="fox_prompt")(qi_tab, ki_tab, kaug, qtaug, vtaug)


def _fox_decode_kernel(pt_ref, q_ref, kn_ref, vn_ref, sm_ref, b_ref, *rest, n_pages, page, ls):
    k_refs, v_refs = rest[0:n_pages], rest[n_pages:2 * n_pages]
    cum_ref, o_ref = rest[2 * n_pages:]
    sample = pl.program_id(0)
    rows = FOX_HEADS * ls
    headmask = _div(_iota((rows, FOX_WIDTH), 0), ls) == _div(_iota((rows, FOX_WIDTH), 1), FOX_HEAD_DIM)
    expand = (_div(_iota((rows, FOX_HEADS), 0), ls) == _iota((rows, FOX_HEADS), 1)).astype(F32)
    q = q_ref[0].astype(F32)
    qbd = jnp.where(headmask, jnp.concatenate([q] * FOX_HEADS, axis=0), 0.0).astype(BF16)

    lane = _iota((ls, LANES), 1)
    lf_new = jnp.where(lane < FOX_HEADS, _log_sigmoid(sm_ref[0] + b_ref[...]), 0.0)
    tok = _iota((ls, LANES), 0)
    cum_new = lf_new
    for s in range(1, ls):
        cum_new = cum_new + jnp.where(tok >= s, pltpu.roll(lf_new, s, 0), 0.0)
    cum_new = cum_new[:, 0:FOX_HEADS]
    query_of_row = (_mod(_iota((rows, ls), 0), ls) == _iota((rows, ls), 1)).astype(F32)
    cq_row = jnp.sum(_dot_hi(query_of_row, cum_new) * expand, axis=1, keepdims=True)

    vis = _iota((rows, ls), 1) <= _mod(_iota((rows, ls), 0), ls)
    tiles = [jnp.where(vis, _dot_nt(qbd, kn_ref[0].astype(BF16)) - _dot_nt_hi(expand, cum_new), -jnp.inf)]
    reach = jnp.zeros((FOX_HEADS, 1), F32)
    key_gate = []
    for r in range(n_pages):
        cum_page = cum_ref[0, pt_ref[sample, n_pages - 1 - r]]
        reach = reach + cum_page[:, page - 1:page]
        key_gate.append(cum_page - reach)
    key_gate = _dot_hi(expand, jnp.concatenate(key_gate, axis=1))
    for r in range(n_pages):
        tiles.append(_dot(qbd, k_refs[r][0, 0].astype(BF16)) - key_gate[:, r * page:(r + 1) * page])

    m = cq_row + functools.reduce(jnp.maximum, [jnp.max(t, axis=1, keepdims=True) for t in tiles])
    shift = cq_row - m
    p_new = jnp.exp(tiles[0] + shift)
    l = jnp.sum(p_new, axis=1, keepdims=True)
    acc = _dot(p_new.astype(BF16), vn_ref[0].astype(BF16))
    for r in range(n_pages):
        p = jnp.exp(tiles[r + 1] + shift)
        l = l + jnp.sum(p, axis=1, keepdims=True)
        acc = acc + _dot_nt(p.astype(BF16), v_refs[r][0, 0].astype(BF16))
    full = jnp.where(headmask, acc / l, 0.0)
    o_ref[0] = jnp.sum(full.reshape(FOX_HEADS, ls, FOX_WIDTH), axis=0)


def _fox_decode(layer, page_table, qb, k_new, v_new, sm, b_pad, cache_k, cache_v, cache_cum):
    bs, ls, _ = qb.shape
    n_pages = page_table.shape[1]
    page = cache_k.shape[3]

    def page_map(r):
        return lambda b, pt: (layer, pt[b, n_pages - 1 - r], 0, 0)

    per_sample = lambda n: pl.BlockSpec((1, ls, n), lambda b, pt: (b, 0, 0))
    in_specs = [per_sample(FOX_WIDTH), per_sample(FOX_WIDTH), per_sample(FOX_WIDTH), per_sample(LANES),
                pl.BlockSpec((1, LANES), lambda b, pt: (0, 0))]
    in_specs += [pl.BlockSpec((1, 1, FOX_WIDTH, page), page_map(r)) for r in range(n_pages)]
    in_specs += [pl.BlockSpec((1, 1, FOX_WIDTH, page), page_map(r)) for r in range(n_pages)]
    in_specs += [pl.BlockSpec((1,) + cache_cum.shape[1:], lambda b, pt: (layer, 0, 0, 0))]
    grid_spec = pltpu.PrefetchScalarGridSpec(
        num_scalar_prefetch=1, grid=(bs,), in_specs=in_specs,
        out_specs=pl.BlockSpec((1, ls, FOX_WIDTH), lambda b, pt: (b, 0, 0)))
    kern = functools.partial(_fox_decode_kernel, n_pages=n_pages, page=page, ls=ls)
    args = [page_table, qb, k_new, v_new, sm, b_pad]
    args += [cache_k] * n_pages + [cache_v] * n_pages + [cache_cum]
    return pl.pallas_call(
        kern, grid_spec=grid_spec, out_shape=jax.ShapeDtypeStruct((bs, ls, FOX_WIDTH), F32),
        compiler_params=_params("arbitrary"), name="fox_decode")(*args)


INV_BASE = 8
GDN_SUB_TILE = 128


def _bf(x):
    return x.astype(BF16)


def _mm(a, b):
    return _dot(_bf(a), _bf(b))


def _split3(x):
    hi = _bf(x)
    r1 = x - hi.astype(F32)
    mid = _bf(r1)
    return hi, mid, _bf(r1 - mid.astype(F32))


def _mm_near_f32(a, b):
    ah, al, _ = _split3(a)
    bh, bl, _ = _split3(b)
    return _dot(ah, bh) + (_dot(ah, bl) + _dot(al, bh))


def _gdn_gates(sm, smt, alog_row, dtb_row, alog_col, dtb_col, cs, sub):
    t_rows = sm.shape[0]
    lg = _log2(cs)
    r = _iota((sub, sub), 0)
    c = _iota((sub, sub), 1)
    same = (r >> lg) == (c >> lg)
    incl = same & (c <= r)
    strict = same & (c < r)
    base = min(INV_BASE, cs)
    base_blk = strict & ((r >> _log2(base)) == (c >> _log2(base)))
    merges = []
    s = base
    while s < cs:
        ls_ = _log2(s)
        merges.append(((r >> (ls_ + 1)) == (c >> (ls_ + 1))) & (((r >> ls_) & 1) == 1) & (((c >> ls_) & 1) == 0))
        s *= 2
    g_col = -jnp.exp(alog_row) * _softplus(sm + dtb_row)
    g_row = -jnp.exp(alog_col) * _softplus(smt + dtb_col)
    incl_b, same_b, upto_b = _bf(incl), _bf(same), _bf(same & (r <= c))
    beta = jax.nn.sigmoid(sm)
    subs = [slice(i, i + sub) for i in range(0, t_rows, sub)]
    gc_col = [sum(_dot(incl_b, part) for part in _split3(g_col[rows])) for rows in subs]
    gl_col = [sum(_dot(same_b, part) for part in _split3(g_col[rows])) for rows in subs]
    gc_row = [sum(_dot(part, upto_b) for part in _split3(g_row[:, rows])) for rows in subs]
    return dict(incl=incl, strict=strict, eye=(r == c).astype(F32), base_blk=base_blk, merges=merges, base=base,
                subs=subs, gc_col=gc_col, gl_col=gl_col, gc_row=gc_row, beta=[beta[rows] for rows in subs])


def _each(f, *lists):
    return [f(*args) for args in zip(*lists)]


def _unit_lower_inverses(a_list, gt):
    powers = _each(lambda a: jnp.where(gt["base_blk"], -a, 0.0), a_list)
    invs = _each(lambda n0: gt["eye"] + n0, powers)
    for _ in range(_log2(gt["base"]) - 1):
        powers = _each(lambda p: _mm(p, p), powers)
        invs = _each(lambda inv, p: inv + _mm(inv, p), invs, powers)
    for merge in gt["merges"]:
        invs_b = _each(_bf, invs)
        lower = _each(lambda inv_b, a: _bf(_dot(inv_b, _bf(jnp.where(merge, a, 0.0)))), invs_b, a_list)
        invs = _each(lambda inv, low, inv_b: inv - _dot(low, inv_b), invs, lower, invs_b)
    return invs


def _gdn_tiles(conv, gt, activated=False):
    hd = GDN_HEAD_DIM
    heads = range(GDN_HEADS)
    head_cols = lambda base, h: slice(base + h * hd, base + (h + 1) * hd)
    if activated:
        q_full = [conv(head_cols(0, h)) for h in heads]
        k_full = [conv(head_cols(GDN_WIDTH, h)) for h in heads]
        v_full = [conv(head_cols(2 * GDN_WIDTH, h)) for h in heads]
    else:
        act = lambda base, h: _silu(conv(head_cols(base, h)))
        l2 = lambda x: x * lax.rsqrt(jnp.sum(x * x, axis=-1, keepdims=True) + EPS)
        q_full = [l2(act(0, h)) * (hd ** -0.5) for h in heads]
        k_full = [l2(act(GDN_WIDTH, h)) for h in heads]
        v_full = [act(2 * GDN_WIDTH, h) for h in heads]
    units = [(s, h) for s in range(len(gt["subs"])) for h in heads]
    q = [q_full[h][gt["subs"][s]] for s, h in units]
    k = [k_full[h][gt["subs"][s]] for s, h in units]
    v = [v_full[h][gt["subs"][s]] for s, h in units]
    beta = [gt["beta"][s][:, GB_COL + h:GB_COL + h + 1] for s, h in units]
    gc = [gt["gc_col"][s][:, GA_COL + h:GA_COL + h + 1] for s, h in units]
    gl = [gt["gl_col"][s][:, GA_COL + h:GA_COL + h + 1] for s, h in units]
    gc_row = [gt["gc_row"][s][GA_COL + h:GA_COL + h + 1, :] for s, h in units]

    decay = _each(lambda c, r: jnp.exp(jnp.where(gt["incl"], c - r, -jnp.inf)), gc, gc_row)
    kb = _each(_bf, k)
    kk = _each(_dot_nt, kb, kb)
    qk = _each(lambda q_, kb_, d: _dot_nt(_bf(q_), kb_) * d, q, kb, decay)
    a = _each(lambda kk_, d, b: jnp.where(gt["strict"], kk_ * d * b, 0.0), kk, decay, beta)
    egc = _each(jnp.exp, gc)
    rhs = _each(lambda v_, k_, b, e: jnp.concatenate([v_ * b, k_ * (b * e)], axis=1), v, k, beta, egc)
    inv_b = _each(_bf, _unit_lower_inverses(a, gt))
    x = _each(lambda i, r: _dot(i, _bf(r)), inv_b, rhs)
    resid = _each(lambda r, x_, a_: _bf(r - (x_ + _mm_near_f32(a_, x_))), rhs, x, a)
    x = _each(lambda x_, i, r: x_ + _dot(i, r), x, inv_b, resid)
    q_dec = _each(lambda q_, e: q_ * e, q, egc)
    k_dec = _each(lambda k_, l_, c: k_ * jnp.exp(l_ - c), k, gl, gc)
    return [x_[:, :hd] for x_ in x], [x_[:, hd:] for x_ in x], qk, q_dec, k_dec, gl


def _gdn_prompt_kernel(qkv_ref, sm_ref, smt_ref, gg_ref, alog_row, dtb_row, alog_col, dtb_col, nw_ref,
                       go_ref, s_out_ref, s_ref, *, cs):
    t = pl.program_id(1)
    t_rows = qkv_ref.shape[0]

    @pl.when(t == 0)
    def _():
        s_ref[...] = jnp.zeros_like(s_ref)

    sub = min(GDN_SUB_TILE, t_rows)
    gt = _gdn_gates(sm_ref[...], smt_ref[...], alog_row[...], dtb_row[...], alog_col[...], dtb_col[...], cs, sub)
    hd = GDN_HEAD_DIM
    heads = range(GDN_HEADS)
    u_base, k_cum, qk, q_dec, k_dec, gl = _gdn_tiles(lambda cols: qkv_ref[:, cols], gt, activated=True)
    states = [s_ref[h] for h in heads]
    for s in range(t_rows // sub):
        unit = lambda h: s * GDN_HEADS + h
        us = [[] for _ in heads]
        ois = [[] for _ in heads]
        for ci in range(sub // cs):
            rows = slice(ci * cs, (ci + 1) * cs)
            sb = _each(_bf, states)
            u = [u_base[unit(h)][rows] - _dot(_bf(k_cum[unit(h)][rows]), sb[h]) for h in heads]
            for h in heads:
                ois[h].append(_dot(_bf(q_dec[unit(h)][rows]), sb[h]))
                us[h].append(u[h])
            states = [states[h] * jnp.exp(gl[unit(h)][ci * cs:ci * cs + 1, :])
                      + _dot_tn(_bf(k_dec[unit(h)][rows]), _bf(u[h])) for h in heads]
        for h in heads:
            o = jnp.concatenate(ois[h], axis=0) + _dot(_bf(qk[unit(h)]), _bf(jnp.concatenate(us[h], axis=0)))
            cols = slice(h * hd, (h + 1) * hd)
            go_ref[gt["subs"][s], cols] = _rms(o, nw_ref[...]) * _silu(gg_ref[gt["subs"][s], cols])
    for h in heads:
        s_ref[h] = states[h]

    @pl.when(t == pl.num_programs(1) - 1)
    def _():
        s_out_ref[0] = s_ref[...]


def _gdn_prompt(qkv, sm, smt, gg, alog_row, dtb_row, alog_col, dtb_col, nw, batch, tile):
    m = qkv.shape[0]
    steps = m // batch // tile
    row = lambda n: pl.BlockSpec((tile, n), lambda b, t: (b * steps + t, 0))
    full = lambda a: pl.BlockSpec(a.shape, lambda b, t: (0,) * a.ndim)
    hd = GDN_HEAD_DIM
    return pl.pallas_call(
        functools.partial(_gdn_prompt_kernel, cs=math.gcd(tile, GDN_CHUNK)), grid=(batch, steps),
        in_specs=[row(GDN_CONV_DIM), row(LANES), pl.BlockSpec((SMALL_COLS, tile), lambda b, t: (0, b * steps + t)),
                  row(GDN_WIDTH), full(alog_row), full(dtb_row), full(alog_col), full(dtb_col), full(nw)],
        out_specs=(row(GDN_WIDTH), pl.BlockSpec((1, GDN_HEADS, hd, hd), lambda b, t: (b, 0, 0, 0))),
        out_shape=(jax.ShapeDtypeStruct((m, GDN_WIDTH), F32),
                   jax.ShapeDtypeStruct((batch, GDN_HEADS, hd, hd), F32)),
        scratch_shapes=[pltpu.VMEM((GDN_HEADS, hd, hd), F32)],
        compiler_params=_params("arbitrary", "arbitrary"), name="gdn_prompt")(
            qkv, sm, smt, gg, alog_row, dtb_row, alog_col, dtb_col, nw)


def _gdn_sample_kernel(qkv_ref, hist_ref, sm_ref, smt_ref, gg_ref, s0_ref, cw_ref, alog_row, dtb_row,
                       alog_col, dtb_col, nw_ref, *rest, ls):
    go_ref, s_out_ref, xbuf_ref = rest[-3:]
    bt = qkv_ref.shape[0]
    xbuf_ref[:, 0:SUBLANES, :] = hist_ref[...]
    xbuf_ref[:, SUBLANES:, :] = qkv_ref[...]

    def conv(cols):
        acc = xbuf_ref[:, SUBLANES:SUBLANES + ls, cols] * cw_ref[CONV_WIDTH - 1:CONV_WIDTH, cols]
        for w in range(CONV_WIDTH - 1):
            lo = SUBLANES - (CONV_WIDTH - 1) + w
            acc = acc + xbuf_ref[:, lo:lo + ls, cols] * cw_ref[w:w + 1, cols]
        return acc.reshape(bt * ls, acc.shape[-1])

    gt = _gdn_gates(sm_ref[...].reshape(bt * ls, LANES), smt_ref[...], alog_row[...], dtb_row[...],
                    alog_col[...], dtb_col[...], ls, bt * ls)
    u_base, k_cum, qk, q_dec, k_dec, gl = _gdn_tiles(conv, gt)
    for h in range(GDN_HEADS):
        rows = [slice(bi * ls, (bi + 1) * ls) for bi in range(bt)]
        states = [s0_ref[0, bi, h] for bi in range(bt)]
        us = [u_base[h][r] - _dot(k_cum[h][r], s) for r, s in zip(rows, states)]
        ois = [_dot(q_dec[h][r], s) for r, s in zip(rows, states)]
        for bi, (r, s, u) in enumerate(zip(rows, states, us)):
            s_out_ref[0, bi, h] = s * jnp.exp(gl[h][bi * ls:bi * ls + 1, :]) + _dot_tn(k_dec[h][r], u)
        o = jnp.concatenate(ois, axis=0) + _dot(_bf(qk[h]), _bf(jnp.concatenate(us, axis=0)))
        hd = GDN_HEAD_DIM
        cols = slice(h * hd, (h + 1) * hd)
        gate = _silu(gg_ref[:, :, cols].reshape(bt * ls, hd))
        go_ref[:, :, cols] = (_rms(o, nw_ref[...]) * gate).reshape(bt, ls, hd)


def _gdn_sample(layer, qkv, hist, sm, smt, gg, s0_all, cw, alog_row, dtb_row, alog_col, dtb_col, nw, prev_states, bt):
    bs, ls, _ = qkv.shape
    per = lambda n: pl.BlockSpec((bt, ls, n), lambda i: (i, 0, 0))
    full = lambda a: pl.BlockSpec(a.shape, lambda i: (0,) * a.ndim)
    hd = GDN_HEAD_DIM
    st = pl.BlockSpec((1, bt, GDN_HEADS, hd, hd), lambda i: (layer, i, 0, 0, 0))
    args = [qkv, hist, sm, smt, gg, s0_all, cw, alog_row, dtb_row, alog_col, dtb_col, nw]
    in_specs = [per(GDN_CONV_DIM), pl.BlockSpec((bt, SUBLANES, GDN_CONV_DIM), lambda i: (i, 0, 0)), per(LANES),
                pl.BlockSpec((SMALL_COLS, bt * ls), lambda i: (0, i)), per(GDN_WIDTH), st,
                full(cw), full(alog_row), full(dtb_row), full(alog_col), full(dtb_col), full(nw)]
    aliases = {}
    if prev_states is not None:
        aliases = {len(args): 1}
        args.append(prev_states)
        in_specs.append(pl.BlockSpec(memory_space=pl.ANY))
    return pl.pallas_call(
        functools.partial(_gdn_sample_kernel, ls=ls), grid=(bs // bt,), in_specs=in_specs,
        out_specs=(per(GDN_WIDTH), st),
        out_shape=(jax.ShapeDtypeStruct((bs, ls, GDN_WIDTH), F32), jax.ShapeDtypeStruct(s0_all.shape, F32)),
        scratch_shapes=[pltpu.VMEM((bt, SUBLANES + ls, GDN_CONV_DIM), F32)],
        input_output_aliases=aliases, compiler_params=_params("arbitrary"), name="gdn_sample")(*args)


def _norm_matmul_kernel(x_ref, g_ref, w_ref, *o_refs):
    xn = _rms(x_ref[...], g_ref[...]).astype(BF16)
    n = w_ref.shape[1] // len(o_refs)
    for i, o_ref in enumerate(o_refs):
        o_ref[...] = _dot(xn, w_ref[:, i * n:(i + 1) * n])


def _norm_matmul(x, g, w, n_out, tm):
    m = x.shape[0]
    n = w.shape[1] // n_out
    return pl.pallas_call(
        _norm_matmul_kernel, grid=(m // tm,),
        in_specs=[pl.BlockSpec((tm, x.shape[1]), lambda i: (i, 0)), pl.BlockSpec(g.shape, lambda i: (0, 0)),
                  pl.BlockSpec(w.shape, lambda i: (0, 0))],
        out_specs=tuple(pl.BlockSpec((tm, n), lambda i: (i, 0)) for _ in range(n_out)),
        out_shape=tuple(jax.ShapeDtypeStruct((m, n), F32) for _ in range(n_out)),
        compiler_params=_params("arbitrary"), name="norm_matmul")(x, g, w)


def _matmul_residual_kernel(*refs, n_in):
    x_ref, o_ref = refs[0], refs[-1]
    acc = x_ref[...]
    for a_ref, w_ref in zip(refs[1:1 + n_in], refs[1 + n_in:1 + 2 * n_in]):
        acc = acc + _dot(a_ref[...].astype(BF16), w_ref[...])
    o_ref[...] = acc


def _matmul_residual(x, a_list, w_list, tm):
    m, d = x.shape
    n_in = len(a_list)
    in_specs = [pl.BlockSpec((tm, d), lambda i: (i, 0))]
    in_specs += [pl.BlockSpec((tm, a.shape[1]), lambda i: (i, 0)) for a in a_list]
    in_specs += [pl.BlockSpec(w.shape, lambda i: (0, 0)) for w in w_list]
    return pl.pallas_call(
        functools.partial(_matmul_residual_kernel, n_in=n_in), grid=(m // tm,), in_specs=in_specs,
        out_specs=pl.BlockSpec((tm, d), lambda i: (i, 0)), out_shape=jax.ShapeDtypeStruct((m, d), F32),
        compiler_params=_params("arbitrary"), name="matmul_residual")(x, *a_list, *w_list)


def _mem_attn_prompt_kernel(x_ref, g_ref, wq_ref, wo_ref, mk_ref, mv_ref, o_ref):
    x = x_ref[0]
    xn = _rms(x, g_ref[...]).astype(BF16)
    q = _dot(xn, wq_ref[...]).astype(BF16)
    cols = [slice(h * MEM_HEAD_DIM, (h + 1) * MEM_HEAD_DIM) for h in range(MEM_HEADS)]
    s = [_dot_nt(q[:, c], mk_ref[0, :, c].astype(BF16)) * (MEM_HEAD_DIM ** -0.5) for c in cols]
    p = [jnp.exp(s_ - jnp.max(s_, axis=1, keepdims=True)) for s_ in s]
    o = [_dot(p_.astype(BF16), mv_ref[0, :, c].astype(BF16)) for p_, c in zip(p, cols)]
    outs = [o_ / jnp.sum(p_, axis=1, keepdims=True) for o_, p_ in zip(o, p)]
    o_ref[0] = x + _dot(jnp.concatenate(outs, axis=1).astype(BF16), wo_ref[...])


def _mem_attn_prompt(x, g, wq, wo, mk, mv, tm):
    b, l, d = x.shape
    n_mem = mk.shape[1]
    full = lambda a: pl.BlockSpec(a.shape, lambda bi, i: (0,) * a.ndim)
    mem_spec = pl.BlockSpec((1, n_mem, MEM_WIDTH), lambda bi, i: (bi, 0, 0))
    x_spec = pl.BlockSpec((1, tm, d), lambda bi, i: (bi, i, 0))
    return pl.pallas_call(
        _mem_attn_prompt_kernel, grid=(b, l // tm),
        in_specs=[x_spec, full(g), full(wq), full(wo), mem_spec, mem_spec],
        out_specs=x_spec, out_shape=jax.ShapeDtypeStruct(x.shape, F32),
        compiler_params=_params("arbitrary", "arbitrary"), name="mem_attn_prompt")(x, g, wq, wo, mk, mv)


def _mem_attn_sample_kernel(q_ref, mk_ref, mv_ref, o_ref):
    bt, ls, _ = q_ref.shape
    rows = MEM_HEADS * ls
    n = mk_ref.shape[2]
    hd = MEM_HEAD_DIM
    own_head = _mod(_iota((rows, n), 1), MEM_HEADS) == _div(_iota((rows, n), 0), ls)
    for bi in range(bt):
        q = q_ref[bi]
        qs = jnp.concatenate([q[:, h * hd:(h + 1) * hd] for h in range(MEM_HEADS)], axis=0)
        s = _dot_nt(qs.astype(BF16), mk_ref[0, bi].astype(BF16)) * (hd ** -0.5)
        s = jnp.where(own_head, s, -jnp.inf)
        p = jnp.exp(s - jnp.max(s, axis=1, keepdims=True))
        o = _dot(p.astype(BF16), mv_ref[0, bi].astype(BF16)) / jnp.sum(p, axis=1, keepdims=True)
        o_ref[bi] = jnp.concatenate([o[h * ls:(h + 1) * ls] for h in range(MEM_HEADS)], axis=1)


def _mem_attn_sample(layer, q, mk_all, mv_all, bt):
    bs, ls, _ = q.shape
    mem_spec = pl.BlockSpec((1, bt) + mk_all.shape[2:], lambda i: (layer, i, 0, 0))
    q_spec = pl.BlockSpec((bt, ls, MEM_WIDTH), lambda i: (i, 0, 0))
    return pl.pallas_call(
        _mem_attn_sample_kernel, grid=(bs // bt,), in_specs=[q_spec, mem_spec, mem_spec],
        out_specs=q_spec, out_shape=jax.ShapeDtypeStruct(q.shape, F32),
        compiler_params=_params("arbitrary"), name="mem_attn_sample")(q, mk_all, mv_all)


def _ffn_kernel(x_ref, g_ref, wa_ref, wu_ref, wo_ref, gf_ref, o_ref, *, chunk, final_norm):
    x = x_ref[...]
    xn = _rms(x, g_ref[...]).astype(BF16)
    n = wa_ref.shape[1] // chunk
    acc = x
    act = None
    for c in range(n + 1):
        if c < n:
            cols = slice(c * chunk, (c + 1) * chunk)
            a = _dot(xn, wa_ref[:, cols])
            u = _dot(xn, wu_ref[:, cols])
        if c > 0:
            acc = acc + _dot(act, wo_ref[(c - 1) * chunk:c * chunk, :])
        if c < n:
            act = (_silu(a) * u).astype(BF16)
    if final_norm:
        acc = _rms(acc, gf_ref[...])
    o_ref[...] = acc


def _ffn(x, g, wa, wu, wo, gf, final_norm, tm):
    m, d = x.shape
    full = lambda a: pl.BlockSpec(a.shape, lambda i: (0,) * a.ndim, pipeline_mode=pl.Buffered(1))
    row = pl.BlockSpec((tm, d), lambda i: (i, 0))
    return pl.pallas_call(
        functools.partial(_ffn_kernel, chunk=2 * LANES, final_norm=final_norm), grid=(m // tm,),
        in_specs=[row, full(g), full(wa), full(wu), full(wo), full(gf)], out_specs=row,
        out_shape=jax.ShapeDtypeStruct((m, d), F32), compiler_params=_params("arbitrary"), name="ffn")(
            x, g, wa, wu, wo, gf)


def _pick(m, cap):
    t = min(m, cap)
    while m % t:
        t -= SUBLANES
    assert t > 0 and m % t == 0, (m, cap)
    return t


def _pad_lanes(v, offset):
    row = jnp.zeros((1, LANES), F32).at[0, offset:offset + v.shape[0]].set(v)
    return row, row[0, :SMALL_COLS].reshape(SMALL_COLS, 1)


def kernel(x_prompt, x_sample, cache_fox_k, cache_fox_v, cache_fox_logf, state_gdn, state_gdn_conv, cache_mem_k, cache_mem_v, page_table, mem_prompt, g_norm_mix, w_in, b_fox_f, gdn_conv_w, gdn_a_log, gdn_dt_bias, gdn_norm_w, w_out, g_norm_memin, w_mem_kv, g_norm_mem, w_mem_q, w_mem_o, g_norm_ffn, w_ffn_in, w_ffn_out, g_final):
    bp, lp, d = x_prompt.shape
    bs, ls, _ = x_sample.shape
    depth = w_in.shape[0]
    n_phys, page = cache_fox_k.shape[1], cache_fox_k.shape[2]
    n_pages = page_table.shape[1]
    n_mem = mem_prompt.shape[1]
    d_ff = w_ffn_out.shape[1]
    mp, ms = bp * lp, bs * ls
    hd = GDN_HEAD_DIM

    yp = x_prompt.reshape(mp, d)
    ys = x_sample.reshape(ms, d)
    mem_flat = mem_prompt.reshape(bp * n_mem, d)
    cache_k = cache_fox_k.transpose(0, 1, 3, 4, 2).reshape(depth, n_phys, FOX_WIDTH, page)
    cache_v = cache_fox_v.transpose(0, 1, 3, 4, 2).reshape(depth, n_phys, FOX_WIDTH, page)
    page_rows = depth * n_phys * FOX_HEADS
    cache_cum = _cumsum_pages(cache_fox_logf.transpose(0, 1, 3, 2).reshape(page_rows, page), _pick(page_rows, 2048))
    cache_cum = cache_cum.reshape(depth, n_phys, FOX_HEADS, page)
    tokens_per_row = LANES // FOX_HEADS
    hist = jnp.pad(state_gdn_conv, ((0, 0), (0, 0), (SUBLANES - (CONV_WIDTH - 1), 0), (0, 0)))
    mem_k_cache = cache_mem_k.reshape(depth, bs, n_mem * MEM_HEADS, MEM_HEAD_DIM)
    mem_v_cache = cache_mem_v.reshape(depth, bs, n_mem * MEM_HEADS, MEM_HEAD_DIM)

    o1 = FOX_WIDTH * 3
    o2 = o1 + FOX_HEADS
    o3 = o2 + GDN_CONV_DIM
    o4 = o3 + 2 * GDN_HEADS
    row2 = lambda v: v.reshape(1, -1)

    outs_p = {k: [] for k in ("lf", "gs", "gc", "mk", "mv")}
    outs_s = {k: [] for k in ("fk", "fv", "lf", "gc")}
    kt_all = vt_all = states_s = None
    for l in range(depth):
        w = w_in[l]
        wq = w[:, :FOX_WIDTH].astype(BF16)
        wqt = w[:, :FOX_WIDTH].T.astype(BF16)
        wkv = w[:, FOX_WIDTH:o1].astype(BF16)
        wk = w[:, FOX_WIDTH:2 * FOX_WIDTH].astype(BF16)
        wkvt = w[:, FOX_WIDTH:o1].T.astype(BF16)
        wg = jnp.concatenate([w[:, o2:o3], w[:, o4:]], axis=1).astype(BF16)
        w_small = jnp.concatenate([w[:, o1:o2], w[:, o3:o4]], axis=1)
        ws = jnp.pad(w_small, ((0, 0), (0, LANES - SMALL_COLS))).astype(BF16)
        wst = w_small.T.astype(BF16)
        b_row = row2(b_fox_f[l])
        b_col = b_fox_f[l].reshape(FOX_HEADS, 1)
        b_pad = jnp.pad(b_row, ((0, 0), (0, LANES - FOX_HEADS)))
        alog_row, alog_col = _pad_lanes(gdn_a_log[l], GA_COL)
        dtb_row, dtb_col = _pad_lanes(gdn_dt_bias[l], GA_COL)
        cw = gdn_conv_w[l]
        nw = row2(gdn_norm_w[l])
        wo_fox = w_out[l][:FOX_WIDTH].astype(BF16)
        wo_gdn = w_out[l][FOX_WIDTH:].astype(BF16)
        w_kv = w_mem_kv[l].astype(BF16)
        w_q = w_mem_q[l].astype(BF16)
        w_o = w_mem_o[l].astype(BF16)
        wa = w_ffn_in[l][:, :d_ff].astype(BF16)
        wu = w_ffn_in[l][:, d_ff:].astype(BF16)
        wf = w_ffn_out[l].astype(BF16)
        g_mix, g_memin, g_mem, g_ffn = row2(g_norm_mix[l]), row2(g_norm_memin[l]), row2(g_norm_mem[l]), row2(g_norm_ffn[l])
        gf = row2(g_final)
        last = l == depth - 1

        kt_all, vt_all, kaug, qtaug, vtaug, qkv, gg, conv_tail, lft, sm, smt = _in_proj_prompt(
            l, depth, bp, yp, g_mix, wqt, wk, wkvt, wg, ws, wst, b_row, b_col, cw, kt_all, vt_all, _pick(lp, 256))
        fo = _fox_prompt(kaug, qtaug, vtaug, _pick(lp, 512)).reshape(mp, FOX_WIDTH)
        go, s_new = _gdn_prompt(qkv, sm, smt, gg, alog_row, dtb_row, alog_col, dtb_col, nw, bp, _pick(lp, 512))
        yp = _matmul_residual(yp, [fo, go], [wo_fox, wo_gdn], _pick(mp, 512))
        mk, mv = _norm_matmul(mem_flat, g_memin, w_kv, 2, _pick(bp * n_mem, 256))
        yp = _mem_attn_prompt(yp.reshape(bp, lp, d), g_mem, w_q, w_o, mk.reshape(bp, n_mem, MEM_WIDTH),
                              mv.reshape(bp, n_mem, MEM_WIDTH), _pick(lp, 512)).reshape(mp, d)
        yp = _ffn(yp, g_ffn, wa, wu, wf, gf, last, _pick(mp, 512))
        outs_p["lf"].append(lft.transpose(0, 2, 1))
        outs_p["gs"].append(s_new)
        outs_p["gc"].append(conv_tail[:, SUBLANES - (CONV_WIDTH - 1):])
        outs_p["mk"].append(mk.reshape(bp, n_mem, MEM_HEADS, MEM_HEAD_DIM))
        outs_p["mv"].append(mv.reshape(bp, n_mem, MEM_HEADS, MEM_HEAD_DIM))

        fqb, fk, fv, qkv, gg, lf, sm, smt = _in_proj_sample(ys, g_mix, wq, wkv, wg, ws, wst, b_row, _pick(ms, 256))
        s3 = lambda a: a.reshape(bs, ls, a.shape[-1])
        fo = _fox_decode(l, page_table, s3(fqb), s3(fk), s3(fv), s3(sm), b_pad, cache_k, cache_v,
                         cache_cum).reshape(ms, FOX_WIDTH)
        go, states_s = _gdn_sample(l, s3(qkv), hist[l], s3(sm), smt, s3(gg), state_gdn, cw, alog_row, dtb_row,
                                   alog_col, dtb_col, nw, states_s, _pick(bs, 16))
        ys = _matmul_residual(ys, [fo, go.reshape(ms, GDN_WIDTH)], [wo_fox, wo_gdn], _pick(ms, 512))
        (q_mem,) = _norm_matmul(ys, g_mem, w_q, 1, _pick(ms, 256))
        o_mem = _mem_attn_sample(l, s3(q_mem), mem_k_cache, mem_v_cache, _pick(bs, 8)).reshape(ms, MEM_WIDTH)
        ys = _matmul_residual(ys, [o_mem], [w_o], _pick(ms, 512))
        ys = _ffn(ys, g_ffn, wa, wu, wf, gf, last, _pick(ms, 256))
        outs_s["fk"].append(fk.reshape(bs, ls, FOX_HEADS, FOX_HEAD_DIM))
        outs_s["fv"].append(fv.reshape(bs, ls, FOX_HEADS, FOX_HEAD_DIM))
        outs_s["lf"].append(lf.reshape(bs, ls, FOX_HEADS))
        outs_s["gc"].append(s3(qkv)[:, ls - (CONV_WIDTH - 1):])

    st = jnp.stack
    heads_last = lambda a: a.reshape(depth, bp, FOX_HEADS, FOX_HEAD_DIM, lp).transpose(0, 1, 4, 2, 3)
    return (yp.reshape(bp, lp, d), ys.reshape(bs, ls, d),
            heads_last(kt_all), heads_last(vt_all), st(outs_p["lf"]), st(outs_p["gs"]), st(outs_p["gc"]),
            st(outs_p["mk"]), st(outs_p["mv"]),
            st(outs_s["fk"]), st(outs_s["fv"]), st(outs_s["lf"]), states_s, st(outs_s["gc"]))
```

```python
import functools
import math

import jax
import jax.numpy as jnp
from jax import lax
from jax.experimental import pallas as pl
from jax.experimental.pallas import tpu as pltpu

F32 = jnp.float32
BF16 = jnp.bfloat16
HI = lax.Precision.HIGHEST

EPS = 1e-6
D_MODEL = 1024
FOX_HEADS = 8
FOX_HEAD_DIM = 64
FOX_WIDTH = FOX_HEADS * FOX_HEAD_DIM
GDN_HEADS = 4
GDN_HEAD_DIM = 128
GDN_WIDTH = GDN_HEADS * GDN_HEAD_DIM
GDN_CONV_DIM = 3 * GDN_WIDTH
CONV_WIDTH = 4
GDN_CHUNK = 64
MEM_HEADS = 4
MEM_HEAD_DIM = 128
MEM_WIDTH = MEM_HEADS * MEM_HEAD_DIM
LANES = 128
SUBLANES = 8
SMALL_COLS = 16
GA_COL = FOX_HEADS
GB_COL = FOX_HEADS + GDN_HEADS
VMEM_LIMIT = 56 * 1024 * 1024


def _params(*sem):
    return pltpu.CompilerParams(dimension_semantics=sem, vmem_limit_bytes=VMEM_LIMIT)


def _dot(a, b):
    return jnp.dot(a, b, preferred_element_type=F32)


def _dot_nt(a, b):
    return lax.dot_general(a, b, (((1,), (1,)), ((), ())), preferred_element_type=F32)


def _dot_tn(a, b):
    return lax.dot_general(a, b, (((0,), (0,)), ((), ())), preferred_element_type=F32)


def _dot_hi(a, b):
    return jnp.dot(a, b, precision=HI, preferred_element_type=F32)


def _dot_nt_hi(a, b):
    return lax.dot_general(a, b, (((1,), (1,)), ((), ())), precision=HI, preferred_element_type=F32)


def _rms(x, g):
    return x * lax.rsqrt(jnp.mean(x * x, axis=-1, keepdims=True) + EPS) * g


def _log_sigmoid(x):
    return jnp.minimum(x, 0.0) - jnp.log1p(jnp.exp(-jnp.abs(x)))


def _softplus(x):
    return jnp.maximum(x, 0.0) + jnp.log1p(jnp.exp(-jnp.abs(x)))


def _silu(x):
    return x * jax.nn.sigmoid(x)


def _iota(shape, dim):
    return lax.broadcasted_iota(jnp.int32, shape, dim)


def _log2(n):
    lg = n.bit_length() - 1
    assert n == 1 << lg, n
    return lg


def _div(x, n):
    return x >> _log2(n)


def _mod(x, n):
    _log2(n)
    return x & (n - 1)


def _in_proj_common(x_ref, g_ref, wq_ref, wg_ref, ws_ref, wst_ref, b_ref,
                    fqb_ref, qkv_ref, gg_ref, lf_ref, sm_ref, smt_ref):
    xn = _rms(x_ref[...], g_ref[...]).astype(BF16)
    fqb_ref[...] = (_dot(xn, wq_ref[...]) * (FOX_HEAD_DIM ** -0.5)).astype(BF16)
    qkv_ref[...] = _dot(xn, wg_ref[:, 0:GDN_CONV_DIM])
    gg_ref[...] = _dot(xn, wg_ref[:, GDN_CONV_DIM:])
    sm = _dot(xn, ws_ref[...])
    smt = _dot_nt(wst_ref[...], xn)
    sm_ref[...] = sm
    smt_ref[...] = smt
    lf_ref[...] = _log_sigmoid(sm[:, 0:FOX_HEADS] + b_ref[...])
    return xn, smt


LOG2E = 1.4426950408889634
AUG_ROWS = SUBLANES
VALUE_ROWS = FOX_HEAD_DIM + 2 * SUBLANES


def _in_proj_prompt_kernel(x_ref, g_ref, wqt_ref, wk_ref, wkvt_ref, wg_ref, ws_ref, wst_ref, b_ref, bcol_ref, cw_ref,
                           *rest, tiles_per_seq):
    (kt_ref, vt_ref, kaug_ref, qtaug_ref, vtaug_ref, qkv_ref, gg_ref, tail_ref, lft_ref, sm_ref, smt_ref,
     crow_ref, ccol_ref, xbuf_ref) = rest[-14:]
    tm = x_ref.shape[0]
    hd = FOX_HEAD_DIM

    @pl.when(pl.program_id(0) % tiles_per_seq == 0)
    def _():
        crow_ref[...] = jnp.zeros_like(crow_ref)
        ccol_ref[...] = jnp.zeros_like(ccol_ref)
        xbuf_ref[0:SUBLANES, :] = jnp.zeros((SUBLANES, GDN_CONV_DIM), F32)

    xn = _rms(x_ref[...], g_ref[...]).astype(BF16)
    sm = _dot(xn, ws_ref[...])
    smt = _dot_nt(wst_ref[...], xn)
    sm_ref[...] = sm
    smt_ref[...] = smt
    lf_col = _log_sigmoid(sm[:, 0:FOX_HEADS] + b_ref[...])
    lf_row = _log_sigmoid(smt[0:FOX_HEADS, :] + bcol_ref[...])
    lft_ref[0] = lf_row
    gdn_cols = wg_ref.shape[1] // FOX_HEADS

    def gdn_slice(i):
        cols = slice(i * gdn_cols, (i + 1) * gdn_cols)
        part = _dot(xn, wg_ref[:, cols])
        if (i + 1) * gdn_cols > GDN_CONV_DIM:
            gg_ref[:, i * gdn_cols - GDN_CONV_DIM:(i + 1) * gdn_cols - GDN_CONV_DIM] = part
            return
        xbuf_ref[SUBLANES:, cols] = part
        conv = part * cw_ref[CONV_WIDTH - 1:CONV_WIDTH, cols]
        for w in range(CONV_WIDTH - 1):
            lo = SUBLANES - (CONV_WIDTH - 1) + w
            conv = conv + xbuf_ref[lo:lo + tm, cols] * cw_ref[w:w + 1, cols]
        act = _silu(conv)
        for c0 in range(i * gdn_cols, (i + 1) * gdn_cols, GDN_HEAD_DIM):
            blk = act[:, c0 - i * gdn_cols:c0 - i * gdn_cols + GDN_HEAD_DIM]
            if c0 < 2 * GDN_WIDTH:
                blk = blk * lax.rsqrt(jnp.sum(blk * blk, axis=-1, keepdims=True) + EPS)
            if c0 < GDN_WIDTH:
                blk = blk * (GDN_HEAD_DIM ** -0.5)
            qkv_ref[:, c0:c0 + GDN_HEAD_DIM] = blk

    kt = _dot_nt(wkvt_ref[0:FOX_WIDTH, :], xn)
    kt_ref[0, 0] = kt
    gdn_slice(0)
    vt = _dot_nt(wkvt_ref[FOX_WIDTH:, :], xn)
    vt_ref[0, 0] = vt
    gdn_slice(1)
    k_tok = _dot(xn, wk_ref[...])
    gdn_slice(2)
    qt = _dot_nt(wqt_ref[...], xn) * (hd ** -0.5 * LOG2E)
    gdn_slice(3)

    r = _iota((tm, tm), 0)
    c = _iota((tm, tm), 1)
    c_col = sum(_dot(_bf(c <= r), part) for part in _split3(lf_col)) + crow_ref[...]
    c_row = sum(_dot(part, _bf(r <= c)) for part in _split3(lf_row)) + ccol_ref[...]
    crow_ref[...] = c_col[tm - 1:tm, :]
    ccol_ref[...] = c_row[:, tm - 1:tm]

    lane = _iota((tm, LANES), 1)
    sub = _iota((AUG_ROWS, tm), 0)
    ones_key = jnp.where((lane >= hd + 3) & (lane < hd + 6), 1.0, 0.0)
    zeros_tail = jnp.zeros((LANES - hd - AUG_ROWS, tm), F32)
    value_tail = jnp.where(_iota((VALUE_ROWS - hd, tm), 0) == 0, 1.0, 0.0)
    for h in range(FOX_HEADS):
        if h + FOX_HEADS // 2 < FOX_HEADS:
            gdn_slice(h + FOX_HEADS // 2)
        pair = k_tok[:, (h // 2) * LANES:(h // 2 + 1) * LANES]
        k_h = pair if h % 2 == 0 else pltpu.roll(pair, hd, 1)
        k1, k2, k3 = [p.astype(F32) for p in _split3(c_col[:, h:h + 1] * (-LOG2E))]
        extra = jnp.where(lane == hd, k1, jnp.where(lane == hd + 1, k2, jnp.where(lane == hd + 2, k3, ones_key)))
        kaug_ref[0, h] = jnp.where(lane < hd, k_h, extra).astype(BF16)
        q1, q2, q3 = [p.astype(F32) for p in _split3(c_row[h:h + 1, :] * LOG2E)]
        query_tail = jnp.where(sub < 3, 1.0, jnp.where(sub == 3, q1, jnp.where(sub == 4, q2,
                                                                           jnp.where(sub == 5, q3, 0.0))))
        rows = slice(h * hd, (h + 1) * hd)
        qtaug_ref[0, h] = jnp.concatenate([qt[rows], query_tail, zeros_tail], axis=0).astype(BF16)
        vtaug_ref[0, h] = jnp.concatenate([vt[rows], value_tail], axis=0).astype(BF16)
    last_rows = xbuf_ref[tm:tm + SUBLANES, :]
    tail_ref[0] = last_rows
    xbuf_ref[0:SUBLANES, :] = last_rows


def _in_proj_prompt(layer, depth, batch, x, g, wqt, wk, wkvt, wg, ws, wst, b_row, b_col, cw, prev_kt, prev_vt, tm):
    m = x.shape[0]
    l = m // batch
    nt = l // tm
    row = lambda n: pl.BlockSpec((tm, n), lambda i: (i, 0))
    full = lambda a: pl.BlockSpec(a.shape, lambda i: (0,) * a.ndim)
    kv_all = pl.BlockSpec((1, 1, FOX_WIDTH, tm), lambda i: (layer, i // nt, 0, i % nt))
    feat_major = pl.BlockSpec((1, FOX_HEADS, LANES, tm), lambda i: (i // nt, 0, 0, i % nt))
    out_shape = (
        jax.ShapeDtypeStruct((depth, batch, FOX_WIDTH, l), F32),
        jax.ShapeDtypeStruct((depth, batch, FOX_WIDTH, l), F32),
        jax.ShapeDtypeStruct((batch, FOX_HEADS, l, LANES), BF16),
        jax.ShapeDtypeStruct((batch, FOX_HEADS, LANES, l), BF16),
        jax.ShapeDtypeStruct((batch, FOX_HEADS, VALUE_ROWS, l), BF16),
        jax.ShapeDtypeStruct((m, GDN_CONV_DIM), F32),
        jax.ShapeDtypeStruct((m, GDN_WIDTH), F32),
        jax.ShapeDtypeStruct((batch, SUBLANES, GDN_CONV_DIM), F32),
        jax.ShapeDtypeStruct((batch, FOX_HEADS, l), F32),
        jax.ShapeDtypeStruct((m, LANES), F32),
        jax.ShapeDtypeStruct((SMALL_COLS, m), F32),
    )
    out_specs = (kv_all, kv_all,
                 pl.BlockSpec((1, FOX_HEADS, tm, LANES), lambda i: (i // nt, 0, i % nt, 0)), feat_major,
                 pl.BlockSpec((1, FOX_HEADS, VALUE_ROWS, tm), lambda i: (i // nt, 0, 0, i % nt)),
                 row(GDN_CONV_DIM), row(GDN_WIDTH),
                 pl.BlockSpec((1, SUBLANES, GDN_CONV_DIM), lambda i: (i // nt, 0, 0)),
                 pl.BlockSpec((1, FOX_HEADS, tm), lambda i: (i // nt, 0, i % nt)), row(LANES),
                 pl.BlockSpec((SMALL_COLS, tm), lambda i: (0, i)))
    args = [x, g, wqt, wk, wkvt, wg, ws, wst, b_row, b_col, cw]
    in_specs = [row(D_MODEL)] + [full(a) for a in args[1:]]
    aliases = {}
    if prev_kt is not None:
        aliases = {len(args): 0, len(args) + 1: 1}
        args += [prev_kt, prev_vt]
        in_specs += [pl.BlockSpec(memory_space=pl.ANY)] * 2
    return pl.pallas_call(
        functools.partial(_in_proj_prompt_kernel, tiles_per_seq=nt), grid=(m // tm,), in_specs=in_specs,
        out_specs=out_specs, out_shape=out_shape, input_output_aliases=aliases,
        scratch_shapes=[pltpu.VMEM((1, FOX_HEADS), F32), pltpu.VMEM((FOX_HEADS, 1), F32),
                        pltpu.VMEM((tm + SUBLANES, GDN_CONV_DIM), F32)],
        compiler_params=_params("arbitrary"), name="in_proj_prompt")(*args)


def _in_proj_sample_kernel(x_ref, g_ref, wq_ref, wkv_ref, wg_ref, ws_ref, wst_ref, b_ref,
                           fqb_ref, fk_ref, fv_ref, qkv_ref, gg_ref, lf_ref, sm_ref, smt_ref):
    xn, _ = _in_proj_common(x_ref, g_ref, wq_ref, wg_ref, ws_ref, wst_ref, b_ref,
                            fqb_ref, qkv_ref, gg_ref, lf_ref, sm_ref, smt_ref)
    fk_ref[...] = _dot(xn, wkv_ref[:, 0:FOX_WIDTH])
    fv_ref[...] = _dot(xn, wkv_ref[:, FOX_WIDTH:])


def _in_proj_sample(x, g, wq, wkv, wg, ws, wst, b_row, tm):
    m = x.shape[0]
    row = lambda n: pl.BlockSpec((tm, n), lambda i: (i, 0))
    full = lambda a: pl.BlockSpec(a.shape, lambda i: (0,) * a.ndim)
    out_shape = (
        jax.ShapeDtypeStruct((m, FOX_WIDTH), BF16),
        jax.ShapeDtypeStruct((m, FOX_WIDTH), F32),
        jax.ShapeDtypeStruct((m, FOX_WIDTH), F32),
        jax.ShapeDtypeStruct((m, GDN_CONV_DIM), F32),
        jax.ShapeDtypeStruct((m, GDN_WIDTH), F32),
        jax.ShapeDtypeStruct((m, FOX_HEADS), F32),
        jax.ShapeDtypeStruct((m, LANES), F32),
        jax.ShapeDtypeStruct((SMALL_COLS, m), F32),
    )
    out_specs = (row(FOX_WIDTH), row(FOX_WIDTH), row(FOX_WIDTH), row(GDN_CONV_DIM), row(GDN_WIDTH),
                 row(FOX_HEADS), row(LANES), pl.BlockSpec((SMALL_COLS, tm), lambda i: (0, i)))
    args = [x, g, wq, wkv, wg, ws, wst, b_row]
    return pl.pallas_call(
        _in_proj_sample_kernel, grid=(m // tm,), in_specs=[row(D_MODEL)] + [full(a) for a in args[1:]],
        out_specs=out_specs, out_shape=out_shape, compiler_params=_params("arbitrary"),
        name="in_proj_sample")(*args)


def _cumsum_pages_kernel(x_ref, o_ref):
    n = x_ref.shape[1]
    upto = _bf(_iota((n, n), 0) <= _iota((n, n), 1))
    o_ref[...] = sum(_dot(part, upto) for part in _split3(x_ref[...]))


def _cumsum_pages(x, block_rows):
    rows, n = x.shape
    spec = pl.BlockSpec((block_rows, n), lambda i: (i, 0))
    return pl.pallas_call(
        _cumsum_pages_kernel, grid=(rows // block_rows,), in_specs=[spec], out_specs=spec,
        out_shape=jax.ShapeDtypeStruct(x.shape, F32), compiler_params=_params("arbitrary"),
        name="cumsum_pages")(x)


FOX_UNIT_Q = 256
FOX_UNIT_GROUP = 8


def _fox_prompt_kernel(qi_ref, ki_ref, k_ref, qt_ref, vt_ref, o_ref, m_ref, acc_ref, *, tq, tk):
    step_id = pl.program_id(1)
    qi = qi_ref[step_id]
    ki = ki_ref[step_id]
    uq = min(FOX_UNIT_Q, tq)
    hd = FOX_HEAD_DIM

    @pl.when(ki == 0)
    def _():
        m_ref[...] = jnp.full_like(m_ref, -jnp.inf)
        acc_ref[...] = jnp.zeros_like(acc_ref)

    units = [(h, qc) for h in range(FOX_HEADS) for qc in range(tq // uq)]

    def step(diagonal):
        for g0 in range(0, len(units), FOX_UNIT_GROUP):
            group = units[g0:g0 + FOX_UNIT_GROUP]
            cols = [slice(qc * uq, (qc + 1) * uq) for _, qc in group]
            keys = [slice(0, (qc + 1) * uq if diagonal else tk) for _, qc in group]
            ts = [_dot(k_ref[0, h, ks], qt_ref[0, h, :, cs_]) for (h, _), cs_, ks in zip(group, cols, keys)]
            if diagonal:
                ts = [jnp.where(_iota(t.shape, 0) <= _iota(t.shape, 1) + qc * uq, t, -jnp.inf)
                      for t, (_, qc) in zip(ts, group)]
            m_old = [m_ref[h, :, cs_] for (h, _), cs_ in zip(group, cols)]
            m_new = [jnp.maximum(mo, jnp.max(t, axis=0, keepdims=True)) for mo, t in zip(m_old, ts)]
            ps = [jnp.exp2(t - mn).astype(BF16) for t, mn in zip(ts, m_new)]
            pvs = [_dot(vt_ref[0, h, :, ks], p) for (h, _), ks, p in zip(group, keys, ps)]
            for (h, _), cs_, mo, mn, pv in zip(group, cols, m_old, m_new, pvs):
                acc_ref[h, :, cs_] = jnp.exp2(mo - mn) * acc_ref[h, :, cs_] + pv
                m_ref[h, :, cs_] = mn

    @pl.when(ki < qi)
    def _():
        step(False)

    @pl.when(ki == qi)
    def _():
        step(True)
        for j in range(FOX_HEADS // 2):
            pair = [acc_ref[h, 0:hd, :] / acc_ref[h, hd:hd + 1, :] for h in (2 * j, 2 * j + 1)]
            o_ref[0, :, j * LANES:(j + 1) * LANES] = jnp.concatenate(pair, axis=0).T


def _fox_prompt(kaug, qtaug, vtaug, tq):
    b, _, l, _ = kaug.shape
    tk = tq
    n = l // tq
    pairs = [(qi, ki) for qi in range(n) for ki in range(qi + 1)]
    qi_tab = jnp.asarray([p[0] for p in pairs], jnp.int32)
    ki_tab = jnp.asarray([p[1] for p in pairs], jnp.int32)
    grid_spec = pltpu.PrefetchScalarGridSpec(
        num_scalar_prefetch=2, grid=(b, len(pairs)),
        in_specs=[pl.BlockSpec((1, FOX_HEADS, tk, LANES), lambda bi, s, qt, kt: (bi, 0, kt[s], 0)),
                  pl.BlockSpec((1, FOX_HEADS, LANES, tq), lambda bi, s, qt, kt: (bi, 0, 0, qt[s])),
                  pl.BlockSpec((1, FOX_HEADS, VALUE_ROWS, tk), lambda bi, s, qt, kt: (bi, 0, 0, kt[s]))],
        out_specs=pl.BlockSpec((1, tq, FOX_WIDTH), lambda bi, s, qt, kt: (bi, qt[s], 0)),
        scratch_shapes=[pltpu.VMEM((FOX_HEADS, 1, tq), F32), pltpu.VMEM((FOX_HEADS, VALUE_ROWS, tq), F32)])
    return pl.pallas_call(
        functools.partial(_fox_prompt_kernel, tq=tq, tk=tk), grid_spec=grid_spec,
        out_shape=jax.ShapeDtypeStruct((b, l, FOX_WIDTH), F32),
        compiler_params=_params("arbitrary", "arbitrary"), name="fox_prompt")(qi_tab, ki_tab, kaug, qtaug, vtaug)


def _fox_decode_kernel(pt_ref, q_ref, kn_ref, vn_ref, sm_ref, b_ref, *rest, n_pages, page, ls):
    k_refs, v_refs = rest[0:n_pages], rest[n_pages:2 * n_pages]
    cum_ref, o_ref = rest[2 * n_pages:]
    sample = pl.program_id(0)
    rows = FOX_HEADS * ls
    headmask = _div(_iota((rows, FOX_WIDTH), 0), ls) == _div(_iota((rows, FOX_WIDTH), 1), FOX_HEAD_DIM)
    expand = (_div(_iota((rows, FOX_HEADS), 0), ls) == _iota((rows, FOX_HEADS), 1)).astype(F32)
    q = q_ref[0].astype(F32)
    qbd = jnp.where(headmask, jnp.concatenate([q] * FOX_HEADS, axis=0), 0.0).astype(BF16)

    lane = _iota((ls, LANES), 1)
    lf_new = jnp.where(lane < FOX_HEADS, _log_sigmoid(sm_ref[0] + b_ref[...]), 0.0)
    tok = _iota((ls, LANES), 0)
    cum_new = lf_new
    for s in range(1, ls):
        cum_new = cum_new + jnp.where(tok >= s, pltpu.roll(lf_new, s, 0), 0.0)
    cum_new = cum_new[:, 0:FOX_HEADS]
    query_of_row = (_mod(_iota((rows, ls), 0), ls) == _iota((rows, ls), 1)).astype(F32)
    cq_row = jnp.sum(_dot_hi(query_of_row, cum_new) * expand, axis=1, keepdims=True)

    vis = _iota((rows, ls), 1) <= _mod(_iota((rows, ls), 0), ls)
    tiles = [jnp.where(vis, _dot_nt(qbd, kn_ref[0].astype(BF16)) - _dot_nt_hi(expand, cum_new), -jnp.inf)]
    reach = jnp.zeros((FOX_HEADS, 1), F32)
    key_gate = []
    for r in range(n_pages):
        cum_page = cum_ref[0, pt_ref[sample, n_pages - 1 - r]]
        reach = reach + cum_page[:, page - 1:page]
        key_gate.append(cum_page - reach)
    key_gate = _dot_hi(expand, jnp.concatenate(key_gate, axis=1))
    for r in range(n_pages):
        tiles.append(_dot(qbd, k_refs[r][0, 0].astype(BF16)) - key_gate[:, r * page:(r + 1) * page])

    m = cq_row + functools.reduce(jnp.maximum, [jnp.max(t, axis=1, keepdims=True) for t in tiles])
    shift = cq_row - m
    p_new = jnp.exp(tiles[0] + shift)
    l = jnp.sum(p_new, axis=1, keepdims=True)
    acc = _dot(p_new.astype(BF16), vn_ref[0].astype(BF16))
    for r in range(n_pages):
        p = jnp.exp(tiles[r + 1] + shift)
        l = l + jnp.sum(p, axis=1, keepdims=True)
        acc = acc + _dot_nt(p.astype(BF16), v_refs[r][0, 0].astype(BF16))
    full = jnp.where(headmask, acc / l, 0.0)
    o_ref[0] = jnp.sum(full.reshape(FOX_HEADS, ls, FOX_WIDTH), axis=0)


def _fox_decode(layer, page_table, qb, k_new, v_new, sm, b_pad, cache_k, cache_v, cache_cum):
    bs, ls, _ = qb.shape
    n_pages = page_table.shape[1]
    page = cache_k.shape[3]

    def page_map(r):
        return lambda b, pt: (layer, pt[b, n_pages - 1 - r], 0, 0)

    per_sample = lambda n: pl.BlockSpec((1, ls, n), lambda b, pt: (b, 0, 0))
    in_specs = [per_sample(FOX_WIDTH), per_sample(FOX_WIDTH), per_sample(FOX_WIDTH), per_sample(LANES),
                pl.BlockSpec((1, LANES), lambda b, pt: (0, 0))]
    in_specs += [pl.BlockSpec((1, 1, FOX_WIDTH, page), page_map(r)) for r in range(n_pages)]
    in_specs += [pl.BlockSpec((1, 1, FOX_WIDTH, page), page_map(r)) for r in range(n_pages)]
    in_specs += [pl.BlockSpec((1,) + cache_cum.shape[1:], lambda b, pt: (layer, 0, 0, 0))]
    grid_spec = pltpu.PrefetchScalarGridSpec(
        num_scalar_prefetch=1, grid=(bs,), in_specs=in_specs,
        out_specs=pl.BlockSpec((1, ls, FOX_WIDTH), lambda b, pt: (b, 0, 0)))
    kern = functools.partial(_fox_decode_kernel, n_pages=n_pages, page=page, ls=ls)
    args = [page_table, qb, k_new, v_new, sm, b_pad]
    args += [cache_k] * n_pages + [cache_v] * n_pages + [cache_cum]
    return pl.pallas_call(
        kern, grid_spec=grid_spec, out_shape=jax.ShapeDtypeStruct((bs, ls, FOX_WIDTH), F32),
        compiler_params=_params("arbitrary"), name="fox_decode")(*args)


INV_BASE = 8
GDN_SUB_TILE = 128


def _bf(x):
    return x.astype(BF16)


def _mm(a, b):
    return _dot(_bf(a), _bf(b))


def _split3(x):
    hi = _bf(x)
    r1 = x - hi.astype(F32)
    mid = _bf(r1)
    return hi, mid, _bf(r1 - mid.astype(F32))


def _mm_near_f32(a, b):
    ah, al, _ = _split3(a)
    bh, bl, _ = _split3(b)
    return _dot(ah, bh) + (_dot(ah, bl) + _dot(al, bh))


def _gdn_gates(sm, smt, alog_row, dtb_row, alog_col, dtb_col, cs, sub):
    t_rows = sm.shape[0]
    lg = _log2(cs)
    r = _iota((sub, sub), 0)
    c = _iota((sub, sub), 1)
    same = (r >> lg) == (c >> lg)
    incl = same & (c <= r)
    strict = same & (c < r)
    base = min(INV_BASE, cs)
    base_blk = strict & ((r >> _log2(base)) == (c >> _log2(base)))
    merges = []
    s = base
    while s < cs:
        ls_ = _log2(s)
        merges.append(((r >> (ls_ + 1)) == (c >> (ls_ + 1))) & (((r >> ls_) & 1) == 1) & (((c >> ls_) & 1) == 0))
        s *= 2
    g_col = -jnp.exp(alog_row) * _softplus(sm + dtb_row)
    g_row = -jnp.exp(alog_col) * _softplus(smt + dtb_col)
    incl_b, same_b, upto_b = _bf(incl), _bf(same), _bf(same & (r <= c))
    beta = jax.nn.sigmoid(sm)
    subs = [slice(i, i + sub) for i in range(0, t_rows, sub)]
    gc_col = [sum(_dot(incl_b, part) for part in _split3(g_col[rows])) for rows in subs]
    gl_col = [sum(_dot(same_b, part) for part in _split3(g_col[rows])) for rows in subs]
    gc_row = [sum(_dot(part, upto_b) for part in _split3(g_row[:, rows])) for rows in subs]
    return dict(incl=incl, strict=strict, eye=(r == c).astype(F32), base_blk=base_blk, merges=merges, base=base,
                subs=subs, gc_col=gc_col, gl_col=gl_col, gc_row=gc_row, beta=[beta[rows] for rows in subs])


def _each(f, *lists):
    return [f(*args) for args in zip(*lists)]


def _unit_lower_inverses(a_list, gt):
    powers = _each(lambda a: jnp.where(gt["base_blk"], -a, 0.0), a_list)
    invs = _each(lambda n0: gt["eye"] + n0, powers)
    for _ in range(_log2(gt["base"]) - 1):
        powers = _each(lambda p: _mm(p, p), powers)
        invs = _each(lambda inv, p: inv + _mm(inv, p), invs, powers)
    for merge in gt["merges"]:
        invs_b = _each(_bf, invs)
        lower = _each(lambda inv_b, a: _bf(_dot(inv_b, _bf(jnp.where(merge, a, 0.0)))), invs_b, a_list)
        invs = _each(lambda inv, low, inv_b: inv - _dot(low, inv_b), invs, lower, invs_b)
    return invs


def _gdn_tiles(conv, gt, activated=False):
    hd = GDN_HEAD_DIM
    heads = range(GDN_HEADS)
    head_cols = lambda base, h: slice(base + h * hd, base + (h + 1) * hd)
    if activated:
        q_full = [conv(head_cols(0, h)) for h in heads]
        k_full = [conv(head_cols(GDN_WIDTH, h)) for h in heads]
        v_full = [conv(head_cols(2 * GDN_WIDTH, h)) for h in heads]
    else:
        act = lambda base, h: _silu(conv(head_cols(base, h)))
        l2 = lambda x: x * lax.rsqrt(jnp.sum(x * x, axis=-1, keepdims=True) + EPS)
        q_full = [l2(act(0, h)) * (hd ** -0.5) for h in heads]
        k_full = [l2(act(GDN_WIDTH, h)) for h in heads]
        v_full = [act(2 * GDN_WIDTH, h) for h in heads]
    units = [(s, h) for s in range(len(gt["subs"])) for h in heads]
    q = [q_full[h][gt["subs"][s]] for s, h in units]
    k = [k_full[h][gt["subs"][s]] for s, h in units]
    v = [v_full[h][gt["subs"][s]] for s, h in units]
    beta = [gt["beta"][s][:, GB_COL + h:GB_COL + h + 1] for s, h in units]
    gc = [gt["gc_col"][s][:, GA_COL + h:GA_COL + h + 1] for s, h in units]
    gl = [gt["gl_col"][s][:, GA_COL + h:GA_COL + h + 1] for s, h in units]
    gc_row = [gt["gc_row"][s][GA_COL + h:GA_COL + h + 1, :] for s, h in units]

    decay = _each(lambda c, r: jnp.exp(jnp.where(gt["incl"], c - r, -jnp.inf)), gc, gc_row)
    kb = _each(_bf, k)
    kk = _each(_dot_nt, kb, kb)
    qk = _each(lambda q_, kb_, d: _dot_nt(_bf(q_), kb_) * d, q, kb, decay)
    a = _each(lambda kk_, d, b: jnp.where(gt["strict"], kk_ * d * b, 0.0), kk, decay, beta)
    egc = _each(jnp.exp, gc)
    rhs = _each(lambda v_, k_, b, e: jnp.concatenate([v_ * b, k_ * (b * e)], axis=1), v, k, beta, egc)
    inv_b = _each(_bf, _unit_lower_inverses(a, gt))
    x = _each(lambda i, r: _dot(i, _bf(r)), inv_b, rhs)
    resid = _each(lambda r, x_, a_: _bf(r - (x_ + _mm_near_f32(a_, x_))), rhs, x, a)
    x = _each(lambda x_, i, r: x_ + _dot(i, r), x, inv_b, resid)
    q_dec = _each(lambda q_, e: q_ * e, q, egc)
    k_dec = _each(lambda k_, l_, c: k_ * jnp.exp(l_ - c), k, gl, gc)
    return [x_[:, :hd] for x_ in x], [x_[:, hd:] for x_ in x], qk, q_dec, k_dec, gl


def _gdn_prompt_kernel(qkv_ref, sm_ref, smt_ref, gg_ref, alog_row, dtb_row, alog_col, dtb_col, nw_ref,
                       go_ref, s_out_ref, s_ref, *, cs):
    t = pl.program_id(1)
    t_rows = qkv_ref.shape[0]

    @pl.when(t == 0)
    def _():
        s_ref[...] = jnp.zeros_like(s_ref)

    sub = min(GDN_SUB_TILE, t_rows)
    gt = _gdn_gates(sm_ref[...], smt_ref[...], alog_row[...], dtb_row[...], alog_col[...], dtb_col[...], cs, sub)
    hd = GDN_HEAD_DIM
    heads = range(GDN_HEADS)
    u_base, k_cum, qk, q_dec, k_dec, gl = _gdn_tiles(lambda cols: qkv_ref[:, cols], gt, activated=True)
    states = [s_ref[h] for h in heads]
    for s in range(t_rows // sub):
        unit = lambda h: s * GDN_HEADS + h
        us = [[] for _ in heads]
        ois = [[] for _ in heads]
        for ci in range(sub // cs):
            rows = slice(ci * cs, (ci + 1) * cs)
            sb = _each(_bf, states)
            u = [u_base[unit(h)][rows] - _dot(_bf(k_cum[unit(h)][rows]), sb[h]) for h in heads]
            for h in heads:
                ois[h].append(_dot(_bf(q_dec[unit(h)][rows]), sb[h]))
                us[h].append(u[h])
            states = [states[h] * jnp.exp(gl[unit(h)][ci * cs:ci * cs + 1, :])
                      + _dot_tn(_bf(k_dec[unit(h)][rows]), _bf(u[h])) for h in heads]
        for h in heads:
            o = jnp.concatenate(ois[h], axis=0) + _dot(_bf(qk[unit(h)]), _bf(jnp.concatenate(us[h], axis=0)))
            cols = slice(h * hd, (h + 1) * hd)
            go_ref[gt["subs"][s], cols] = _rms(o, nw_ref[...]) * _silu(gg_ref[gt["subs"][s], cols])
    for h in heads:
        s_ref[h] = states[h]

    @pl.when(t == pl.num_programs(1) - 1)
    def _():
        s_out_ref[0] = s_ref[...]


def _gdn_prompt(qkv, sm, smt, gg, alog_row, dtb_row, alog_col, dtb_col, nw, batch, tile):
    m = qkv.shape[0]
    steps = m // batch // tile
    row = lambda n: pl.BlockSpec((tile, n), lambda b, t: (b * steps + t, 0))
    full = lambda a: pl.BlockSpec(a.shape, lambda b, t: (0,) * a.ndim)
    hd = GDN_HEAD_DIM
    return pl.pallas_call(
        functools.partial(_gdn_prompt_kernel, cs=math.gcd(tile, GDN_CHUNK)), grid=(batch, steps),
        in_specs=[row(GDN_CONV_DIM), row(LANES), pl.BlockSpec((SMALL_COLS, tile), lambda b, t: (0, b * steps + t)),
                  row(GDN_WIDTH), full(alog_row), full(dtb_row), full(alog_col), full(dtb_col), full(nw)],
        out_specs=(row(GDN_WIDTH), pl.BlockSpec((1, GDN_HEADS, hd, hd), lambda b, t: (b, 0, 0, 0))),
        out_shape=(jax.ShapeDtypeStruct((m, GDN_WIDTH), F32),
                   jax.ShapeDtypeStruct((batch, GDN_HEADS, hd, hd), F32)),
        scratch_shapes=[pltpu.VMEM((GDN_HEADS, hd, hd), F32)],
        compiler_params=_params("arbitrary", "arbitrary"), name="gdn_prompt")(
            qkv, sm, smt, gg, alog_row, dtb_row, alog_col, dtb_col, nw)


def _gdn_sample_kernel(qkv_ref, hist_ref, sm_ref, smt_ref, gg_ref, s0_ref, cw_ref, alog_row, dtb_row,
                       alog_col, dtb_col, nw_ref, *rest, ls):
    go_ref, s_out_ref, xbuf_ref = rest[-3:]
    bt = qkv_ref.shape[0]
    xbuf_ref[:, 0:SUBLANES, :] = hist_ref[...]
    xbuf_ref[:, SUBLANES:, :] = qkv_ref[...]

    def conv(cols):
        acc = xbuf_ref[:, SUBLANES:SUBLANES + ls, cols] * cw_ref[CONV_WIDTH - 1:CONV_WIDTH, cols]
        for w in range(CONV_WIDTH - 1):
            lo = SUBLANES - (CONV_WIDTH - 1) + w
            acc = acc + xbuf_ref[:, lo:lo + ls, cols] * cw_ref[w:w + 1, cols]
        return acc.reshape(bt * ls, acc.shape[-1])

    gt = _gdn_gates(sm_ref[...].reshape(bt * ls, LANES), smt_ref[...], alog_row[...], dtb_row[...],
                    alog_col[...], dtb_col[...], ls, bt * ls)
    u_base, k_cum, qk, q_dec, k_dec, gl = _gdn_tiles(conv, gt)
    for h in range(GDN_HEADS):
        rows = [slice(bi * ls, (bi + 1) * ls) for bi in range(bt)]
        states = [s0_ref[0, bi, h] for bi in range(bt)]
        us = [u_base[h][r] - _dot(k_cum[h][r], s) for r, s in zip(rows, states)]
        ois = [_dot(q_dec[h][r], s) for r, s in zip(rows, states)]
        for bi, (r, s, u) in enumerate(zip(rows, states, us)):
            s_out_ref[0, bi, h] = s * jnp.exp(gl[h][bi * ls:bi * ls + 1, :]) + _dot_tn(k_dec[h][r], u)
        o = jnp.concatenate(ois, axis=0) + _dot(_bf(qk[h]), _bf(jnp.concatenate(us, axis=0)))
        hd = GDN_HEAD_DIM
        cols = slice(h * hd, (h + 1) * hd)
        gate = _silu(gg_ref[:, :, cols].reshape(bt * ls, hd))
        go_ref[:, :, cols] = (_rms(o, nw_ref[...]) * gate).reshape(bt, ls, hd)


def _gdn_sample(layer, qkv, hist, sm, smt, gg, s0_all, cw, alog_row, dtb_row, alog_col, dtb_col, nw, prev_states, bt):
    bs, ls, _ = qkv.shape
    per = lambda n: pl.BlockSpec((bt, ls, n), lambda i: (i, 0, 0))
    full = lambda a: pl.BlockSpec(a.shape, lambda i: (0,) * a.ndim)
    hd = GDN_HEAD_DIM
    st = pl.BlockSpec((1, bt, GDN_HEADS, hd, hd), lambda i: (layer, i, 0, 0, 0))
    args = [qkv, hist, sm, smt, gg, s0_all, cw, alog_row, dtb_row, alog_col, dtb_col, nw]
    in_specs = [per(GDN_CONV_DIM), pl.BlockSpec((bt, SUBLANES, GDN_CONV_DIM), lambda i: (i, 0, 0)), per(LANES),
                pl.BlockSpec((SMALL_COLS, bt * ls), lambda i: (0, i)), per(GDN_WIDTH), st,
                full(cw), full(alog_row), full(dtb_row), full(alog_col), full(dtb_col), full(nw)]
    aliases = {}
    if prev_states is not None:
        aliases = {len(args): 1}
        args.append(prev_states)
        in_specs.append(pl.BlockSpec(memory_space=pl.ANY))
    return pl.pallas_call(
        functools.partial(_gdn_sample_kernel, ls=ls), grid=(bs // bt,), in_specs=in_specs,
        out_specs=(per(GDN_WIDTH), st),
        out_shape=(jax.ShapeDtypeStruct((bs, ls, GDN_WIDTH), F32), jax.ShapeDtypeStruct(s0_all.shape, F32)),
        scratch_shapes=[pltpu.VMEM((bt, SUBLANES + ls, GDN_CONV_DIM), F32)],
        input_output_aliases=aliases, compiler_params=_params("arbitrary"), name="gdn_sample")(*args)


def _norm_matmul_kernel(x_ref, g_ref, w_ref, *o_refs):
    xn = _rms(x_ref[...], g_ref[...]).astype(BF16)
    n = w_ref.shape[1] // len(o_refs)
    for i, o_ref in enumerate(o_refs):
        o_ref[...] = _dot(xn, w_ref[:, i * n:(i + 1) * n])


def _norm_matmul(x, g, w, n_out, tm):
    m = x.shape[0]
    n = w.shape[1] // n_out
    return pl.pallas_call(
        _norm_matmul_kernel, grid=(m // tm,),
        in_specs=[pl.BlockSpec((tm, x.shape[1]), lambda i: (i, 0)), pl.BlockSpec(g.shape, lambda i: (0, 0)),
                  pl.BlockSpec(w.shape, lambda i: (0, 0))],
        out_specs=tuple(pl.BlockSpec((tm, n), lambda i: (i, 0)) for _ in range(n_out)),
        out_shape=tuple(jax.ShapeDtypeStruct((m, n), F32) for _ in range(n_out)),
        compiler_params=_params("arbitrary"), name="norm_matmul")(x, g, w)


def _matmul_residual_kernel(*refs, n_in):
    x_ref, o_ref = refs[0], refs[-1]
    acc = x_ref[...]
    for a_ref, w_ref in zip(refs[1:1 + n_in], refs[1 + n_in:1 + 2 * n_in]):
        acc = acc + _dot(a_ref[...].astype(BF16), w_ref[...])
    o_ref[...] = acc


def _matmul_residual(x, a_list, w_list, tm):
    m, d = x.shape
    n_in = len(a_list)
    in_specs = [pl.BlockSpec((tm, d), lambda i: (i, 0))]
    in_specs += [pl.BlockSpec((tm, a.shape[1]), lambda i: (i, 0)) for a in a_list]
    in_specs += [pl.BlockSpec(w.shape, lambda i: (0, 0)) for w in w_list]
    return pl.pallas_call(
        functools.partial(_matmul_residual_kernel, n_in=n_in), grid=(m // tm,), in_specs=in_specs,
        out_specs=pl.BlockSpec((tm, d), lambda i: (i, 0)), out_shape=jax.ShapeDtypeStruct((m, d), F32),
        compiler_params=_params("arbitrary"), name="matmul_residual")(x, *a_list, *w_list)


def _mem_attn_prompt_kernel(x_ref, g_ref, wq_ref, wo_ref, mk_ref, mv_ref, o_ref):
    x = x_ref[0]
    xn = _rms(x, g_ref[...]).astype(BF16)
    q = _dot(xn, wq_ref[...]).astype(BF16)
    cols = [slice(h * MEM_HEAD_DIM, (h + 1) * MEM_HEAD_DIM) for h in range(MEM_HEADS)]
    s = [_dot_nt(q[:, c], mk_ref[0, :, c].astype(BF16)) * (MEM_HEAD_DIM ** -0.5) for c in cols]
    p = [jnp.exp(s_ - jnp.max(s_, axis=1, keepdims=True)) for s_ in s]
    o = [_dot(p_.astype(BF16), mv_ref[0, :, c].astype(BF16)) for p_, c in zip(p, cols)]
    outs = [o_ / jnp.sum(p_, axis=1, keepdims=True) for o_, p_ in zip(o, p)]
    o_ref[0] = x + _dot(jnp.concatenate(outs, axis=1).astype(BF16), wo_ref[...])


def _mem_attn_prompt(x, g, wq, wo, mk, mv, tm):
    b, l, d = x.shape
    n_mem = mk.shape[1]
    full = lambda a: pl.BlockSpec(a.shape, lambda bi, i: (0,) * a.ndim)
    mem_spec = pl.BlockSpec((1, n_mem, MEM_WIDTH), lambda bi, i: (bi, 0, 0))
    x_spec = pl.BlockSpec((1, tm, d), lambda bi, i: (bi, i, 0))
    return pl.pallas_call(
        _mem_attn_prompt_kernel, grid=(b, l // tm),
        in_specs=[x_spec, full(g), full(wq), full(wo), mem_spec, mem_spec],
        out_specs=x_spec, out_shape=jax.ShapeDtypeStruct(x.shape, F32),
        compiler_params=_params("arbitrary", "arbitrary"), name="mem_attn_prompt")(x, g, wq, wo, mk, mv)


def _mem_attn_sample_kernel(q_ref, mk_ref, mv_ref, o_ref):
    bt, ls, _ = q_ref.shape
    rows = MEM_HEADS * ls
    n = mk_ref.shape[2]
    hd = MEM_HEAD_DIM
    own_head = _mod(_iota((rows, n), 1), MEM_HEADS) == _div(_iota((rows, n), 0), ls)
    for bi in range(bt):
        q = q_ref[bi]
        qs = jnp.concatenate([q[:, h * hd:(h + 1) * hd] for h in range(MEM_HEADS)], axis=0)
        s = _dot_nt(qs.astype(BF16), mk_ref[0, bi].astype(BF16)) * (hd ** -0.5)
        s = jnp.where(own_head, s, -jnp.inf)
        p = jnp.exp(s - jnp.max(s, axis=1, keepdims=True))
        o = _dot(p.astype(BF16), mv_ref[0, bi].astype(BF16)) / jnp.sum(p, axis=1, keepdims=True)
        o_ref[bi] = jnp.concatenate([o[h * ls:(h + 1) * ls] for h in range(MEM_HEADS)], axis=1)


def _mem_attn_sample(layer, q, mk_all, mv_all, bt):
    bs, ls, _ = q.shape
    mem_spec = pl.BlockSpec((1, bt) + mk_all.shape[2:], lambda i: (layer, i, 0, 0))
    q_spec = pl.BlockSpec((bt, ls, MEM_WIDTH), lambda i: (i, 0, 0))
    return pl.pallas_call(
        _mem_attn_sample_kernel, grid=(bs // bt,), in_specs=[q_spec, mem_spec, mem_spec],
        out_specs=q_spec, out_shape=jax.ShapeDtypeStruct(q.shape, F32),
        compiler_params=_params("arbitrary"), name="mem_attn_sample")(q, mk_all, mv_all)


def _ffn_kernel(x_ref, g_ref, wa_ref, wu_ref, wo_ref, gf_ref, o_ref, *, chunk, final_norm):
    x = x_ref[...]
    xn = _rms(x, g_ref[...]).astype(BF16)
    n = wa_ref.shape[1] // chunk
    acc = x
    act = None
    for c in range(n + 1):
        if c < n:
            cols = slice(c * chunk, (c + 1) * chunk)
            a = _dot(xn, wa_ref[:, cols])
            u = _dot(xn, wu_ref[:, cols])
        if c > 0:
            acc = acc + _dot(act, wo_ref[(c - 1) * chunk:c * chunk, :])
        if c < n:
            act = (_silu(a) * u).astype(BF16)
    if final_norm:
        acc = _rms(acc, gf_ref[...])
    o_ref[...] = acc


def _ffn(x, g, wa, wu, wo, gf, final_norm, tm):
    m, d = x.shape
    full = lambda a: pl.BlockSpec(a.shape, lambda i: (0,) * a.ndim, pipeline_mode=pl.Buffered(1))
    row = pl.BlockSpec((tm, d), lambda i: (i, 0))
    return pl.pallas_call(
        functools.partial(_ffn_kernel, chunk=2 * LANES, final_norm=final_norm), grid=(m // tm,),
        in_specs=[row, full(g), full(wa), full(wu), full(wo), full(gf)], out_specs=row,
        out_shape=jax.ShapeDtypeStruct((m, d), F32), compiler_params=_params("arbitrary"), name="ffn")(
            x, g, wa, wu, wo, gf)


def _pick(m, cap):
    t = min(m, cap)
    while m % t:
        t -= SUBLANES
    assert t > 0 and m % t == 0, (m, cap)
    return t


def _pad_lanes(v, offset):
    row = jnp.zeros((1, LANES), F32).at[0, offset:offset + v.shape[0]].set(v)
    return row, row[0, :SMALL_COLS].reshape(SMALL_COLS, 1)


def kernel(x_prompt, x_sample, cache_fox_k, cache_fox_v, cache_fox_logf, state_gdn, state_gdn_conv, cache_mem_k, cache_mem_v, page_table, mem_prompt, g_norm_mix, w_in, b_fox_f, gdn_conv_w, gdn_a_log, gdn_dt_bias, gdn_norm_w, w_out, g_norm_memin, w_mem_kv, g_norm_mem, w_mem_q, w_mem_o, g_norm_ffn, w_ffn_in, w_ffn_out, g_final):
    bp, lp, d = x_prompt.shape
    bs, ls, _ = x_sample.shape
    depth = w_in.shape[0]
    n_phys, page = cache_fox_k.shape[1], cache_fox_k.shape[2]
    n_pages = page_table.shape[1]
    n_mem = mem_prompt.shape[1]
    d_ff = w_ffn_out.shape[1]
    mp, ms = bp * lp, bs * ls
    hd = GDN_HEAD_DIM

    yp = x_prompt.reshape(mp, d)
    ys = x_sample.reshape(ms, d)
    mem_flat = mem_prompt.reshape(bp * n_mem, d)
    cache_k = cache_fox_k.transpose(0, 1, 3, 4, 2).reshape(depth, n_phys, FOX_WIDTH, page)
    cache_v = cache_fox_v.transpose(0, 1, 3, 4, 2).reshape(depth, n_phys, FOX_WIDTH, page)
    page_rows = depth * n_phys * FOX_HEADS
    cache_cum = _cumsum_pages(cache_fox_logf.transpose(0, 1, 3, 2).reshape(page_rows, page), _pick(page_rows, 2048))
    cache_cum = cache_cum.reshape(depth, n_phys, FOX_HEADS, page)
    tokens_per_row = LANES // FOX_HEADS
    hist = jnp.pad(state_gdn_conv, ((0, 0), (0, 0), (SUBLANES - (CONV_WIDTH - 1), 0), (0, 0)))
    mem_k_cache = cache_mem_k.reshape(depth, bs, n_mem * MEM_HEADS, MEM_HEAD_DIM)
    mem_v_cache = cache_mem_v.reshape(depth, bs, n_mem * MEM_HEADS, MEM_HEAD_DIM)

    o1 = FOX_WIDTH * 3
    o2 = o1 + FOX_HEADS
    o3 = o2 + GDN_CONV_DIM
    o4 = o3 + 2 * GDN_HEADS
    row2 = lambda v: v.reshape(1, -1)

    outs_p = {k: [] for k in ("lf", "gs", "gc", "mk", "mv")}
    outs_s = {k: [] for k in ("fk", "fv", "lf", "gc")}
    kt_all = vt_all = states_s = None
    for l in range(depth):
        w = w_in[l]
        wq = w[:, :FOX_WIDTH].astype(BF16)
        wqt = w[:, :FOX_WIDTH].T.astype(BF16)
        wkv = w[:, FOX_WIDTH:o1].astype(BF16)
        wk = w[:, FOX_WIDTH:2 * FOX_WIDTH].astype(BF16)
        wkvt = w[:, FOX_WIDTH:o1].T.astype(BF16)
        wg = jnp.concatenate([w[:, o2:o3], w[:, o4:]], axis=1).astype(BF16)
        w_small = jnp.concatenate([w[:, o1:o2], w[:, o3:o4]], axis=1)
        ws = jnp.pad(w_small, ((0, 0), (0, LANES - SMALL_COLS))).astype(BF16)
        wst = w_small.T.astype(BF16)
        b_row = row2(b_fox_f[l])
        b_col = b_fox_f[l].reshape(FOX_HEADS, 1)
        b_pad = jnp.pad(b_row, ((0, 0), (0, LANES - FOX_HEADS)))
        alog_row, alog_col = _pad_lanes(gdn_a_log[l], GA_COL)
        dtb_row, dtb_col = _pad_lanes(gdn_dt_bias[l], GA_COL)
        cw = gdn_conv_w[l]
        nw = row2(gdn_norm_w[l])
        wo_fox = w_out[l][:FOX_WIDTH].astype(BF16)
        wo_gdn = w_out[l][FOX_WIDTH:].astype(BF16)
        w_kv = w_mem_kv[l].astype(BF16)
        w_q = w_mem_q[l].astype(BF16)
        w_o = w_mem_o[l].astype(BF16)
        wa = w_ffn_in[l][:, :d_ff].astype(BF16)
        wu = w_ffn_in[l][:, d_ff:].astype(BF16)
        wf = w_ffn_out[l].astype(BF16)
        g_mix, g_memin, g_mem, g_ffn = row2(g_norm_mix[l]), row2(g_norm_memin[l]), row2(g_norm_mem[l]), row2(g_norm_ffn[l])
        gf = row2(g_final)
        last = l == depth - 1

        kt_all, vt_all, kaug, qtaug, vtaug, qkv, gg, conv_tail, lft, sm, smt = _in_proj_prompt(
            l, depth, bp, yp, g_mix, wqt, wk, wkvt, wg, ws, wst, b_row, b_col, cw, kt_all, vt_all, _pick(lp, 256))
        fo = _fox_prompt(kaug, qtaug, vtaug, _pick(lp, 1024)).reshape(mp, FOX_WIDTH)
        go, s_new = _gdn_prompt(qkv, sm, smt, gg, alog_row, dtb_row, alog_col, dtb_col, nw, bp, _pick(lp, 512))
        yp = _matmul_residual(yp, [fo, go], [wo_fox, wo_gdn], _pick(mp, 512))
        mk, mv = _norm_matmul(mem_flat, g_memin, w_kv, 2, _pick(bp * n_mem, 256))
        yp = _mem_attn_prompt(yp.reshape(bp, lp, d), g_mem, w_q, w_o, mk.reshape(bp, n_mem, MEM_WIDTH),
                              mv.reshape(bp, n_mem, MEM_WIDTH), _pick(lp, 512)).reshape(mp, d)
        yp = _ffn(yp, g_ffn, wa, wu, wf, gf, last, _pick(mp, 512))
        outs_p["lf"].append(lft.transpose(0, 2, 1))
        outs_p["gs"].append(s_new)
        outs_p["gc"].append(conv_tail[:, SUBLANES - (CONV_WIDTH - 1):])
        outs_p["mk"].append(mk.reshape(bp, n_mem, MEM_HEADS, MEM_HEAD_DIM))
        outs_p["mv"].append(mv.reshape(bp, n_mem, MEM_HEADS, MEM_HEAD_DIM))

        fqb, fk, fv, qkv, gg, lf, sm, smt = _in_proj_sample(ys, g_mix, wq, wkv, wg, ws, wst, b_row, _pick(ms, 256))
        s3 = lambda a: a.reshape(bs, ls, a.shape[-1])
        fo = _fox_decode(l, page_table, s3(fqb), s3(fk), s3(fv), s3(sm), b_pad, cache_k, cache_v,
                         cache_cum).reshape(ms, FOX_WIDTH)
        go, states_s = _gdn_sample(l, s3(qkv), hist[l], s3(sm), smt, s3(gg), state_gdn, cw, alog_row, dtb_row,
                                   alog_col, dtb_col, nw, states_s, _pick(bs, 16))
        ys = _matmul_residual(ys, [fo, go.reshape(ms, GDN_WIDTH)], [wo_fox, wo_gdn], _pick(ms, 512))
        (q_mem,) = _norm_matmul(ys, g_mem, w_q, 1, _pick(ms, 256))
        o_mem = _mem_attn_sample(l, s3(q_mem), mem_k_cache, mem_v_cache, _pick(bs, 8)).reshape(ms, MEM_WIDTH)
        ys = _matmul_residual(ys, [o_mem], [w_o], _pick(ms, 512))
        ys = _ffn(ys, g_ffn, wa, wu, wf, gf, last, _pick(ms, 256))
        outs_s["fk"].append(fk.reshape(bs, ls, FOX_HEADS, FOX_HEAD_DIM))
        outs_s["fv"].append(fv.reshape(bs, ls, FOX_HEADS, FOX_HEAD_DIM))
        outs_s["lf"].append(lf.reshape(bs, ls, FOX_HEADS))
        outs_s["gc"].append(s3(qkv)[:, ls - (CONV_WIDTH - 1):])

    st = jnp.stack
    heads_last = lambda a: a.reshape(depth, bp, FOX_HEADS, FOX_HEAD_DIM, lp).transpose(0, 1, 4, 2, 3)
    return (yp.reshape(bp, lp, d), ys.reshape(bs, ls, d),
            heads_last(kt_all), heads_last(vt_all), st(outs_p["lf"]), st(outs_p["gs"]), st(outs_p["gc"]),
            st(outs_p["mk"]), st(outs_p["mv"]),
            st(outs_s["fk"]), st(outs_s["fv"]), st(outs_s["lf"]), states_s, st(outs_s["gc"]))
```
